```python
import jax, jax.numpy as jnp
from jax import lax
import numpy as np

D_MODEL = 1024
BATCH = 16
SEQ = 2048
DEPTH = 1

N_MEM = 256
EPS = 1e-6
NEG_INF = -1e30
D_FF = 2816
N_HEADS_A = 8
D_QK = 64
D_V = 64
Q_LORA = 256
KV_LORA = 256
N_IDX = 4
D_IDX = 64
TOPK_MAX = 256
Q_BLOCK = 128
CONV_CH = 512
CONV_WIDTH = 31
N_HEADS_M = 4
D_HEAD_M = 128
N_BRANCH = 3
SPLITS = (Q_LORA, KV_LORA, D_IDX, N_IDX, 2 * CONV_CH, N_HEADS_M * D_HEAD_M, N_BRANCH * D_MODEL)
D_IN = sum(SPLITS)

kernel_name = "hybrid_dsa_conformer_memory_block"


def rms_norm(x, g):
    xf = x.astype(jnp.float32)
    y = xf * lax.rsqrt(jnp.mean(xf * xf, axis=-1, keepdims=True) + EPS)
    return (y * g.astype(jnp.float32)).astype(x.dtype)


def layer_norm(x, g, b):
    xf = x.astype(jnp.float32)
    mu = jnp.mean(xf, axis=-1, keepdims=True)
    xc = xf - mu
    var = jnp.mean(xc * xc, axis=-1, keepdims=True)
    return (xc * lax.rsqrt(var + EPS) * g.astype(jnp.float32) + b.astype(jnp.float32)).astype(x.dtype)


def half_step_ffn(x, g_pre, g_post, w_gu, w_down):
    gate, up = jnp.split(rms_norm(x, g_pre) @ w_gu, 2, axis=-1)
    return x + 0.5 * rms_norm((jax.nn.silu(gate) * up) @ w_down, g_post)


def dsa_attention(h, c_q, c_kv, k_idx_raw, w_idx_raw, q_norm_g, kv_norm_g, w_uq, w_uk, w_uv,
                  w_idx_q, idx_ln_g, idx_ln_b):
    B, T, _ = h.shape
    topk = min(TOPK_MAX, T // 4)
    cq = rms_norm(c_q, q_norm_g)
    ckv = rms_norm(c_kv, kv_norm_g)
    q = (cq @ w_uq).reshape(B, T, N_HEADS_A, D_QK)
    q_lat = jnp.einsum('bthd,chd->bthc', q, w_uk) * (D_QK ** -0.5)
    q_idx = ((cq @ w_idx_q) * (D_IDX ** -0.5)).reshape(B, T, N_IDX, D_IDX).astype(jnp.float32)
    k_idx = layer_norm(k_idx_raw, idx_ln_g, idx_ln_b).astype(jnp.float32)
    w_idx = (w_idx_raw * (N_IDX ** -0.5)).astype(jnp.float32)
    key_pos = jnp.arange(T)

    def block(start):
        qi = lax.dynamic_slice_in_dim(q_idx, start, Q_BLOCK, axis=1)
        wi = lax.dynamic_slice_in_dim(w_idx, start, Q_BLOCK, axis=1)
        ql = lax.dynamic_slice_in_dim(q_lat, start, Q_BLOCK, axis=1)
        qpos = start + jnp.arange(Q_BLOCK)
        logits = jnp.einsum('bqhd,bsd->bqhs', qi, k_idx)
        score = jnp.einsum('bqh,bqhs->bqs', wi, jax.nn.relu(logits))
        causal = key_pos[None, :] <= qpos[:, None]
        score = jnp.where(causal[None], score, NEG_INF)
        _, sel = lax.top_k(score, topk)
        valid = sel <= qpos[None, :, None]
        kv_sel = jax.vmap(lambda c, i: c[i])(ckv, sel)
        s = jnp.einsum('bqhc,bqkc->bqhk', ql, kv_sel).astype(jnp.float32)
        s = jnp.where(valid[:, :, None, :], s, NEG_INF)
        p = jax.nn.softmax(s, axis=-1).astype(kv_sel.dtype)
        o_lat = jnp.einsum('bqhk,bqkc->bqhc', p, kv_sel)
        o = jnp.einsum('bqhc,chd->bqhd', o_lat, w_uv)
        return o.reshape(B, Q_BLOCK, N_HEADS_A * D_V)

    starts = jnp.arange(T // Q_BLOCK) * Q_BLOCK
    out = lax.map(block, starts)
    return out.transpose(1, 0, 2, 3).reshape(B, T, N_HEADS_A * D_V)


def conv_module(u_glu, w_dw, b_dw, ln_g, ln_b, w_pw_out):
    a, g = jnp.split(u_glu, 2, axis=-1)
    u = a * jax.nn.sigmoid(g)
    y = lax.conv_general_dilated(
        u, w_dw[:, None, :].astype(u.dtype), window_strides=(1,),
        padding=((CONV_WIDTH - 1, 0),), dimension_numbers=('NWC', 'WIO', 'NWC'),
        feature_group_count=CONV_CH) + b_dw
    return jax.nn.silu(layer_norm(y, ln_g, ln_b)) @ w_pw_out


def memory_attention(q_raw, mem, mem_norm_g, w_mem_kv):
    B, T, _ = q_raw.shape
    M = mem.shape[1]
    q = q_raw.reshape(B, T, N_HEADS_M, D_HEAD_M)
    k, v = jnp.split(rms_norm(mem, mem_norm_g) @ w_mem_kv, 2, axis=-1)
    k = k.reshape(B, M, N_HEADS_M, D_HEAD_M)
    v = v.reshape(B, M, N_HEADS_M, D_HEAD_M)
    s = jnp.einsum('bthd,bmhd->bhtm', q, k).astype(jnp.float32) * (D_HEAD_M ** -0.5)
    p = jax.nn.softmax(s, axis=-1).astype(v.dtype)
    return jnp.einsum('bhtm,bmhd->bthd', p, v).reshape(B, T, N_HEADS_M * D_HEAD_M)


def hybrid_mixer(x, mem, mix_pre_g, mix_post_g, w_in,
                 q_norm_g, kv_norm_g, w_uq, w_uk, w_uv, w_idx_q, idx_ln_g, idx_ln_b, w_dsa_o,
                 w_dw, b_dw, conv_ln_g, conv_ln_b, w_conv_out,
                 mem_norm_g, w_mem_kv, w_mem_o, w_out):
    h = rms_norm(x, mix_pre_g)
    cuts = [int(c) for c in np.cumsum(SPLITS)[:-1]]
    c_q, c_kv, k_idx_raw, w_idx_raw, u_glu, q_mem, gate_logits = jnp.split(h @ w_in, cuts, axis=-1)
    y_a = dsa_attention(h, c_q, c_kv, k_idx_raw, w_idx_raw, q_norm_g, kv_norm_g, w_uq, w_uk, w_uv,
                        w_idx_q, idx_ln_g, idx_ln_b) @ w_dsa_o
    y_b = conv_module(u_glu, w_dw, b_dw, conv_ln_g, conv_ln_b, w_conv_out)
    y_m = memory_attention(q_mem, mem, mem_norm_g, w_mem_kv) @ w_mem_o
    g_a, g_b, g_m = jnp.split(jax.nn.sigmoid(gate_logits), N_BRANCH, axis=-1)
    merged = g_a * y_a + g_b * y_b + g_m * y_m
    return x + rms_norm(merged @ w_out, mix_post_g)


def setup_inputs(seed: int = 0) -> dict:
    key = jax.random.key(seed)
    ks = iter(jax.random.split(key, 40))

    def dense(shape, fan_in):
        return jax.random.normal(next(ks), (DEPTH,) + shape, jnp.float32) * (fan_in ** -0.5)

    def gain(n):
        return 1.0 + 0.02 * jax.random.normal(next(ks), (DEPTH, n), jnp.float32)

    def bias(n):
        return 0.02 * jax.random.normal(next(ks), (DEPTH, n), jnp.float32)

    x = jax.random.normal(next(ks), (BATCH, SEQ, D_MODEL), jnp.float32)
    mem = jax.random.normal(next(ks), (BATCH, N_MEM, D_MODEL), jnp.float32)
    return {
        "x": x, "mem": mem,
        "ffn1_pre_g": gain(D_MODEL), "ffn1_post_g": gain(D_MODEL),
        "ffn1_w_gu": dense((D_MODEL, 2 * D_FF), D_MODEL), "ffn1_w_down": dense((D_FF, D_MODEL), D_FF),
        "mix_pre_g": gain(D_MODEL), "mix_post_g": gain(D_MODEL),
        "w_in": dense((D_MODEL, D_IN), D_MODEL),
        "q_norm_g": gain(Q_LORA), "kv_norm_g": gain(KV_LORA),
        "w_uq": dense((Q_LORA, N_HEADS_A * D_QK), Q_LORA),
        "w_uk": dense((KV_LORA, N_HEADS_A, D_QK), KV_LORA),
        "w_uv": dense((KV_LORA, N_HEADS_A, D_V), KV_LORA),
        "w_idx_q": dense((Q_LORA, N_IDX * D_IDX), Q_LORA),
        "idx_ln_g": gain(D_IDX), "idx_ln_b": bias(D_IDX),
        "w_dsa_o": dense((N_HEADS_A * D_V, D_MODEL), N_HEADS_A * D_V),
        "w_dw": dense((CONV_WIDTH, CONV_CH), CONV_WIDTH), "b_dw": bias(CONV_CH),
        "conv_ln_g": gain(CONV_CH), "conv_ln_b": bias(CONV_CH),
        "w_conv_out": dense((CONV_CH, D_MODEL), CONV_CH),
        "mem_norm_g": gain(D_MODEL),
        "w_mem_kv": dense((D_MODEL, 2 * N_HEADS_M * D_HEAD_M), D_MODEL),
        "w_mem_o": dense((N_HEADS_M * D_HEAD_M, D_MODEL), N_HEADS_M * D_HEAD_M),
        "w_out": dense((D_MODEL, D_MODEL), D_MODEL),
        "ffn2_pre_g": gain(D_MODEL), "ffn2_post_g": gain(D_MODEL),
        "ffn2_w_gu": dense((D_MODEL, 2 * D_FF), D_MODEL), "ffn2_w_down": dense((D_FF, D_MODEL), D_FF),
    }


def reference(x, mem, ffn1_pre_g, ffn1_post_g, ffn1_w_gu, ffn1_w_down,
              mix_pre_g, mix_post_g, w_in,
              q_norm_g, kv_norm_g, w_uq, w_uk, w_uv, w_idx_q, idx_ln_g, idx_ln_b, w_dsa_o,
              w_dw, b_dw, conv_ln_g, conv_ln_b, w_conv_out,
              mem_norm_g, w_mem_kv, w_mem_o, w_out,
              ffn2_pre_g, ffn2_post_g, ffn2_w_gu, ffn2_w_down):
    for l in range(DEPTH):
        x = half_step_ffn(x, ffn1_pre_g[l], ffn1_post_g[l], ffn1_w_gu[l], ffn1_w_down[l])
        x = hybrid_mixer(x, mem, mix_pre_g[l], mix_post_g[l], w_in[l],
                         q_norm_g[l], kv_norm_g[l], w_uq[l], w_uk[l], w_uv[l], w_idx_q[l],
                         idx_ln_g[l], idx_ln_b[l], w_dsa_o[l],
                         w_dw[l], b_dw[l], conv_ln_g[l], conv_ln_b[l], w_conv_out[l],
                         mem_norm_g[l], w_mem_kv[l], w_mem_o[l], w_out[l])
        x = half_step_ffn(x, ffn2_pre_g[l], ffn2_post_g[l], ffn2_w_gu[l], ffn2_w_down[l])
    return x
```

```python
import functools
import math

import jax
import jax.numpy as jnp
from jax import lax
from jax.experimental import pallas as pl
from jax.experimental.pallas import tpu as pltpu

F32 = jnp.float32
BF16 = jnp.bfloat16
I32 = jnp.int32

EPS = 1e-6
NEG_INF = -1e30
D_MODEL = 1024
D_FF = 2816
N_HEADS_A = 8
D_QK = 64
D_V = 64
Q_LORA = 256
KV_LORA = 256
N_IDX = 4
D_IDX = 64
TOPK_MAX = 256
CONV_CH = 512
CONV_WIDTH = 31
N_HEADS_M = 4
D_HEAD_M = 128
N_MEM = 256

LANES = 128
SUBLANES = 8
BF16_ROWS = 16
VMEM_LIMIT = 56 * 1024 * 1024
INT_MIN = -2147483648
LOG2E = math.log2(math.e)

FFN_TM = 512
FFN_CHUNKS = ((0, 1536), (1536, 1280))
PRE_TM = 512
HALO = 32
TQ = 256
CK = 256
CNT_ROWS = 32


def _rms(x, g):
    return x * lax.rsqrt(jnp.mean(x * x, axis=-1, keepdims=True) + EPS) * g


def _layer_norm(x, g, b):
    mu = jnp.mean(x, axis=-1, keepdims=True)
    xc = x - mu
    var = jnp.mean(xc * xc, axis=-1, keepdims=True)
    return xc * lax.rsqrt(var + EPS) * g + b


def _dot(a, b):
    return jnp.dot(a, b, preferred_element_type=F32)


def _dot_nt(a, b):
    return lax.dot_general(a, b, (((1,), (1,)), ((), ())), preferred_element_type=F32)


def _const_spec(shape):
    nd = len(shape)
    return pl.BlockSpec(shape, lambda *_: (0,) * nd, pipeline_mode=pl.Buffered(1))


def _ffn_body(x, gpre_ref, gpost_ref, wg_ref, wu_ref, wd_ref):
    h = _rms(x, gpre_ref[...]).astype(BF16)
    acc = None
    for start, size in FFN_CHUNKS:
        g = _dot(h, wg_ref[:, start:start + size])
        u = _dot(h, wu_ref[:, start:start + size])
        a = (g * jax.nn.sigmoid(g) * u).astype(BF16)
        part = _dot(a, wd_ref[start:start + size, :])
        acc = part if acc is None else acc + part
    return x + 0.5 * _rms(acc, gpost_ref[...])


def _ffn_kernel(x_ref, gpre_ref, gpost_ref, wg_ref, wu_ref, wd_ref, o_ref):
    o_ref[...] = _ffn_body(x_ref[...], gpre_ref, gpost_ref, wg_ref, wu_ref, wd_ref)


def _post_ffn_kernel(x_ref, oa_ref, ga_ref, mbm_ref, wdo_ref, wout_ref, gmix_ref,
                     gpre_ref, gpost_ref, wg_ref, wu_ref, wd_ref, o_ref):
    y_a = _dot(oa_ref[...], wdo_ref[...])
    merged = ga_ref[...].astype(F32) * y_a + mbm_ref[...].astype(F32)
    x = x_ref[...] + _rms(_dot(merged.astype(BF16), wout_ref[...]), gmix_ref[...])
    o_ref[...] = _ffn_body(x, gpre_ref, gpost_ref, wg_ref, wu_ref, wd_ref)


def _ffn_weights(g_pre, g_post, w_gu, w_down):
    d = g_pre.shape[-1]
    return [g_pre.reshape(1, d), g_post.reshape(1, d), w_gu[:, :D_FF].astype(BF16),
            w_gu[:, D_FF:].astype(BF16), w_down.astype(BF16)]


def _ffn(x2d, g_pre, g_post, w_gu, w_down):
    n, d = x2d.shape
    assert n % FFN_TM == 0 and d == D_MODEL
    consts = _ffn_weights(g_pre, g_post, w_gu, w_down)
    row = pl.BlockSpec((FFN_TM, d), lambda i: (i, 0))
    return pl.pallas_call(
        _ffn_kernel,
        grid=(n // FFN_TM,),
        in_specs=[row] + [_const_spec(c.shape) for c in consts],
        out_specs=row,
        out_shape=jax.ShapeDtypeStruct((n, d), F32),
        compiler_params=pltpu.CompilerParams(
            dimension_semantics=("arbitrary",), vmem_limit_bytes=VMEM_LIMIT),
        name="ffn",
    )(x2d, *consts)


def _post_ffn(x2d, oa2d, ga2d, mbm2d, p, g_pre, g_post, w_gu, w_down):
    n, d = x2d.shape
    assert n % FFN_TM == 0 and d == D_MODEL
    consts = [p["w_dsa_o"], p["w_out"], p["mix_post_g"]] + _ffn_weights(g_pre, g_post, w_gu, w_down)

    def row(width):
        return pl.BlockSpec((FFN_TM, width), lambda i: (i, 0))

    return pl.pallas_call(
        _post_ffn_kernel,
        grid=(n // FFN_TM,),
        in_specs=[row(d), row(oa2d.shape[1]), row(d), row(d)] + [_const_spec(c.shape) for c in consts],
        out_specs=row(d),
        out_shape=jax.ShapeDtypeStruct((n, d), F32),
        compiler_params=pltpu.CompilerParams(
            dimension_semantics=("arbitrary",), vmem_limit_bytes=VMEM_LIMIT),
        name="post_ffn",
    )(x2d, oa2d, ga2d, mbm2d, *consts)


def _mix_pre_kernel(x_ref, mem_ref, gpre_ref, wcq_ref, wckv_ref, wsm_ref, wglu_ref, wqm_ref,
                    wgate_ref, qng_ref, kvng_ref, ilg_ref, ilb_ref, wdw_ref, bdw_ref, clg_ref,
                    clb_ref, wco_ref, mng_ref, wmkv_ref, wmo_ref,
                    cq_ref, ckv_ref, ckvt_ref, kidx_ref, widxt_ref, ga_ref, mbm_ref,
                    ubuf_ref, kmem_ref, vmem_ref):
    j = pl.program_id(1)
    tm = x_ref.shape[1]
    hm = N_HEADS_M * D_HEAD_M

    @pl.when(j == 0)
    def _():
        ubuf_ref[0:HALO, :] = jnp.zeros((HALO, CONV_CH), F32)
        mk = _rms(mem_ref[0], mng_ref[...]).astype(BF16)
        kv = _dot(mk, wmkv_ref[...])
        kmem_ref[...] = kv[:, :hm].astype(BF16)
        vmem_ref[...] = kv[:, hm:].astype(BF16)

    h = _rms(x_ref[0], gpre_ref[...]).astype(BF16)

    cq_ref[0] = _rms(_dot(h, wcq_ref[...]), qng_ref[...]).astype(BF16)
    ckv = _rms(_dot(h, wckv_ref[...]), kvng_ref[...])
    ckv_ref[0] = ckv.astype(BF16)
    for k in range(tm // CK):
        ckvt_ref[0, k, 0:KV_LORA, :] = ckv[k * CK:(k + 1) * CK, :].T.astype(BF16)
        ckvt_ref[0, k, KV_LORA:KV_LORA + BF16_ROWS, :] = jnp.ones((BF16_ROWS, CK), BF16)
    small = _dot(h, wsm_ref[...])
    kidx_ref[0] = _layer_norm(small[:, :D_IDX], ilg_ref[...], ilb_ref[...]).astype(BF16)
    small_t = small.T
    widxt_ref[0] = small_t[D_IDX:D_IDX + SUBLANES, :] * (N_IDX ** -0.5)

    uglu = _dot(h, wglu_ref[...])
    ubuf_ref[HALO:HALO + tm, :] = uglu[:, :CONV_CH] * jax.nn.sigmoid(uglu[:, CONV_CH:])
    first = HALO - (CONV_WIDTH - 1)
    y = bdw_ref[...] + wdw_ref[0:1, :] * ubuf_ref[first:first + tm, :]
    for t in range(1, CONV_WIDTH):
        y = y + wdw_ref[t:t + 1, :] * ubuf_ref[first + t:first + t + tm, :]
    ubuf_ref[0:HALO, :] = ubuf_ref[tm:tm + HALO, :]
    z = _layer_norm(y, clg_ref[...], clb_ref[...])
    y_b = _dot((z * jax.nn.sigmoid(z)).astype(BF16), wco_ref[...])

    qm = _dot(h, wqm_ref[...]).astype(BF16)
    heads = []
    for hh in range(N_HEADS_M):
        sl = slice(hh * D_HEAD_M, (hh + 1) * D_HEAD_M)
        s = _dot_nt(qm[:, sl], kmem_ref[:, sl]) * (D_HEAD_M ** -0.5)
        p = jnp.exp(s - jnp.max(s, axis=-1, keepdims=True))
        o = _dot(p.astype(BF16), vmem_ref[:, sl])
        heads.append((o / jnp.sum(p, axis=-1, keepdims=True)).astype(BF16))
    y_m = _dot(jnp.concatenate(heads, axis=-1), wmo_ref[...])

    gates = jax.nn.sigmoid(_dot(h, wgate_ref[...]))
    ga_ref[0] = gates[:, :D_MODEL].astype(BF16)
    mbm_ref[0] = (gates[:, D_MODEL:2 * D_MODEL] * y_b + gates[:, 2 * D_MODEL:] * y_m).astype(BF16)


def _mix_pre(x, mem, p):
    b, t, d = x.shape
    tm = min(PRE_TM, t)
    assert t % tm == 0 and tm % CK == 0 and d == D_MODEL and mem.shape == (b, N_MEM, d)
    hm = N_HEADS_M * D_HEAD_M

    def tile(width):
        return pl.BlockSpec((1, tm, width), lambda i, j: (i, j, 0))

    consts = [p["mix_pre_g"], p["w_cq"], p["w_ckv"], p["w_small"], p["w_glu"], p["w_qmem"],
              p["w_gate"], p["q_norm_g"], p["kv_norm_g"], p["idx_ln_g"], p["idx_ln_b"], p["w_dw"],
              p["b_dw"], p["conv_ln_g"], p["conv_ln_b"], p["w_conv_out"], p["mem_norm_g"],
              p["w_mem_kv"], p["w_mem_o"]]
    kvt_rows = KV_LORA + BF16_ROWS
    out_shape = (
        jax.ShapeDtypeStruct((b, t, Q_LORA), BF16),
        jax.ShapeDtypeStruct((b, t, KV_LORA), BF16),
        jax.ShapeDtypeStruct((b, t // CK, kvt_rows, CK), BF16),
        jax.ShapeDtypeStruct((b, t, D_IDX), BF16),
        jax.ShapeDtypeStruct((b, SUBLANES, t), F32),
        jax.ShapeDtypeStruct((b, t, d), BF16),
        jax.ShapeDtypeStruct((b, t, d), BF16),
    )
    out_specs = (
        tile(Q_LORA), tile(KV_LORA),
        pl.BlockSpec((1, tm // CK, kvt_rows, CK), lambda i, j: (i, j, 0, 0)),
        tile(D_IDX),
        pl.BlockSpec((1, SUBLANES, tm), lambda i, j: (i, 0, j)),
        tile(d), tile(d),
    )
    return pl.pallas_call(
        _mix_pre_kernel,
        grid=(b, t // tm),
        in_specs=[tile(d), pl.BlockSpec((1, N_MEM, d), lambda i, j: (i, 0, 0))]
        + [_const_spec(c.shape) for c in consts],
        out_specs=out_specs,
        out_shape=out_shape,
        scratch_shapes=[pltpu.VMEM((HALO + tm, CONV_CH), F32),
                        pltpu.VMEM((N_MEM, hm), BF16), pltpu.VMEM((N_MEM, hm), BF16)],
        compiler_params=pltpu.CompilerParams(
            dimension_semantics=("arbitrary", "arbitrary"), vmem_limit_bytes=VMEM_LIMIT),
        name="mix_pre",
    )(x, mem, *consts)


def _key_to_f32(key):
    return pltpu.bitcast(jnp.where(key < 0, key ^ 0x7FFFFFFF, key), F32)


def _mix_dsa_kernel(topk, cq_ref, widxt_ref, ckv_ref, ckvt_ref, kidx_ref,
                    wiqt_ref, wuqt_ref, wuk_ref, wuvt_ref, ltri_ref,
                    o_ref, sc_ref, acc_ref):
    qi = pl.program_id(1)
    nch = (qi * TQ + TQ + CK - 1) // CK

    cq = cq_ref[0]
    qpos = qi * TQ + lax.broadcasted_iota(I32, (CK, TQ), 1)
    krow = lax.broadcasted_iota(I32, (CK, TQ), 0)

    qidx_t = (_dot_nt(wiqt_ref[...], cq) * (D_IDX ** -0.5)).astype(BF16)
    widx_t = widxt_ref[0]

    def score_body(c, carry):
        start = pl.multiple_of(c * CK, CK)
        kc = kidx_ref[0, pl.ds(start, CK), :]
        score = None
        for hh in range(N_IDX):
            lg = _dot(kc, qidx_t[hh * D_IDX:(hh + 1) * D_IDX, :])
            term = widx_t[hh:hh + 1, :] * jnp.maximum(lg, 0.0)
            score = term if score is None else score + term
        sc_ref[pl.ds(start, CK), :] = jnp.where(krow + start <= qpos, score + 0.0, NEG_INF)
        return carry

    lax.fori_loop(0, nch, score_body, 0)

    def count(pred):
        def body(c, cnt):
            start = pl.multiple_of(c * CK, CK)
            hit = jnp.where(pred(sc_ref[pl.ds(start, CK), :]), 1, 0)
            return cnt + hit.reshape(CK // CNT_ROWS, CNT_ROWS, TQ).sum(axis=0)
        cnt = lax.fori_loop(0, nch, body, jnp.zeros((CNT_ROWS, TQ), I32))
        return jnp.sum(cnt, axis=0, keepdims=True)

    def max_below(bound):
        def body(c, best):
            start = pl.multiple_of(c * CK, CK)
            s = sc_ref[pl.ds(start, CK), :]
            cand = jnp.where(s < bound, s, -jnp.inf)
            return jnp.maximum(best, cand.reshape(CK // CNT_ROWS, CNT_ROWS, TQ).max(axis=0))
        best = lax.fori_loop(0, nch, body, jnp.full((CNT_ROWS, TQ), -jnp.inf, F32))
        return jnp.max(best, axis=0, keepdims=True)

    def bit_body(i, prefix):
        cand = prefix | (jnp.int32(1) << (31 - i))
        thr = _key_to_f32(cand ^ INT_MIN)
        return jnp.where(count(lambda s: s >= thr) >= topk, cand, prefix)

    lo_key = lax.fori_loop(0, 32, bit_body, jnp.zeros((1, TQ), I32)) ^ INT_MIN
    v1 = max_below(_key_to_f32(lo_key + 1))
    kth = jnp.where(count(lambda s: s >= v1) >= topk, v1, max_below(v1))
    n_tie = (topk - count(lambda s: s > kth)).astype(F32)

    q_t = _dot_nt(wuqt_ref[...], cq).astype(BF16)
    qlat = [(_dot(wuk_ref[hh], q_t[hh * D_QK:(hh + 1) * D_QK, :]) * (D_QK ** -0.5 * LOG2E)).astype(BF16)
            for hh in range(N_HEADS_A)]

    acc_ref[...] = jnp.zeros_like(acc_ref)

    def attn_body(c, carry):
        m_run, l_run, tie_seen = carry
        start = pl.multiple_of(c * CK, CK)
        sc = sc_ref[pl.ds(start, CK), :]
        tie = sc == kth
        tie_rank = _dot(ltri_ref[...], jnp.where(tie, 1.0, 0.0).astype(BF16)) + tie_seen
        bias = jnp.where(sc > kth, 0.0,
                         jnp.where(tie, jnp.where(tie_rank <= n_tie, 0.0, NEG_INF), NEG_INF))
        bias = jnp.where(krow + start <= qpos, bias, NEG_INF)
        kc = ckv_ref[0, pl.ds(start, CK), :]
        kct = ckvt_ref[0, c]
        m_out, l_out = [], []
        for hh in range(N_HEADS_A):
            s = _dot(kc, qlat[hh]) + bias
            m_new = jnp.maximum(m_run[hh], jnp.max(s, axis=0, keepdims=True))
            p = jnp.exp2(s - jnp.maximum(m_new, 0.1 * NEG_INF)).astype(BF16)
            alpha = jnp.exp2(m_run[hh] - m_new)
            pv = _dot(kct, p)
            acc_ref[hh] = alpha * acc_ref[hh] + pv[0:KV_LORA, :]
            l_out.append(alpha * l_run[hh] + pv[KV_LORA:KV_LORA + 1, :])
            m_out.append(m_new)
        return tuple(m_out), tuple(l_out), tie_rank[CK - 1:CK, :]

    init = (tuple(jnp.full((1, TQ), NEG_INF, F32) for _ in range(N_HEADS_A)),
            tuple(jnp.zeros((1, TQ), F32) for _ in range(N_HEADS_A)),
            jnp.zeros((1, TQ), F32))
    _, l_fin, _ = lax.fori_loop(0, nch, attn_body, init)

    o_t = jnp.concatenate(
        [_dot(wuvt_ref[hh], (acc_ref[hh] * (1.0 / l_fin[hh])).astype(BF16)) for hh in range(N_HEADS_A)],
        axis=0)
    o_ref[0] = o_t.T.astype(BF16)


def _mix_dsa(pre, p, t):
    cq, ckv, ckvt, kidx, widxt = pre
    b = cq.shape[0]
    assert t % CK == 0 and t % TQ == 0
    topk = min(TOPK_MAX, t // 4)
    assert topk <= CK
    ltri = (lax.broadcasted_iota(I32, (CK, CK), 1) <= lax.broadcasted_iota(I32, (CK, CK), 0)).astype(BF16)
    consts = [p["w_idx_q_t"], p["w_uq_t"], p["w_uk_h"], p["w_uv_t"], ltri]

    def whole(shape):
        nd = len(shape)
        return pl.BlockSpec((1,) + tuple(shape[1:]), lambda i, j: (i,) + (0,) * (nd - 1))

    return pl.pallas_call(
        functools.partial(_mix_dsa_kernel, topk),
        grid=(b, t // TQ),
        in_specs=[pl.BlockSpec((1, TQ, Q_LORA), lambda i, j: (i, j, 0)),
                  pl.BlockSpec((1, SUBLANES, TQ), lambda i, j: (i, 0, j)),
                  whole(ckv.shape), whole(ckvt.shape), whole(kidx.shape)]
        + [_const_spec(c.shape) for c in consts],
        out_specs=pl.BlockSpec((1, TQ, N_HEADS_A * D_V), lambda i, j: (i, j, 0)),
        out_shape=jax.ShapeDtypeStruct((b, t, N_HEADS_A * D_V), BF16),
        scratch_shapes=[pltpu.VMEM((t, TQ), F32), pltpu.VMEM((N_HEADS_A, KV_LORA, TQ), F32)],
        compiler_params=pltpu.CompilerParams(
            dimension_semantics=("arbitrary", "arbitrary"), vmem_limit_bytes=VMEM_LIMIT),
        name="mix_dsa",
    )(cq, widxt, ckv, ckvt, kidx, *consts)


def _mixer_params(mix_pre_g, mix_post_g, w_in, q_norm_g, kv_norm_g, w_uq, w_uk, w_uv, w_idx_q,
                  idx_ln_g, idx_ln_b, w_dsa_o, w_dw, b_dw, conv_ln_g, conv_ln_b, w_conv_out,
                  mem_norm_g, w_mem_kv, w_mem_o, w_out):
    hm = N_HEADS_M * D_HEAD_M
    cuts = [0]
    for width in (Q_LORA, KV_LORA, D_IDX, N_IDX, 2 * CONV_CH, hm, 3 * D_MODEL):
        cuts.append(cuts[-1] + width)
    assert w_in.shape == (D_MODEL, cuts[-1])
    w_small = jnp.pad(w_in[:, cuts[2]:cuts[4]], ((0, 0), (0, LANES - D_IDX - N_IDX)))
    row = lambda v: v.reshape(1, -1).astype(F32)
    return {
        "mix_pre_g": row(mix_pre_g), "mix_post_g": row(mix_post_g),
        "w_cq": w_in[:, cuts[0]:cuts[1]].astype(BF16), "w_ckv": w_in[:, cuts[1]:cuts[2]].astype(BF16),
        "w_small": w_small.astype(BF16),
        "w_glu": w_in[:, cuts[4]:cuts[5]].astype(BF16), "w_qmem": w_in[:, cuts[5]:cuts[6]].astype(BF16),
        "w_gate": w_in[:, cuts[6]:cuts[7]].astype(BF16),
        "q_norm_g": row(q_norm_g), "kv_norm_g": row(kv_norm_g),
        "idx_ln_g": row(idx_ln_g), "idx_ln_b": row(idx_ln_b),
        "w_dw": w_dw.astype(F32), "b_dw": row(b_dw),
        "conv_ln_g": row(conv_ln_g), "conv_ln_b": row(conv_ln_b),
        "w_conv_out": w_conv_out.astype(BF16), "mem_norm_g": row(mem_norm_g),
        "w_mem_kv": w_mem_kv.astype(BF16), "w_mem_o": w_mem_o.astype(BF16),
        "w_idx_q_t": w_idx_q.T.astype(BF16), "w_uq_t": w_uq.T.astype(BF16),
        "w_uk_h": jnp.transpose(w_uk, (1, 0, 2)).astype(BF16),
        "w_uv_t": jnp.transpose(w_uv, (1, 2, 0)).astype(BF16),
        "w_dsa_o": w_dsa_o.astype(BF16), "w_out": w_out.astype(BF16),
    }


def kernel(x, mem, ffn1_pre_g, ffn1_post_g, ffn1_w_gu, ffn1_w_down, mix_pre_g, mix_post_g, w_in, q_norm_g, kv_norm_g, w_uq, w_uk, w_uv, w_idx_q, idx_ln_g, idx_ln_b, w_dsa_o, w_dw, b_dw, conv_ln_g, conv_ln_b, w_conv_out, mem_norm_g, w_mem_kv, w_mem_o, w_out, ffn2_pre_g, ffn2_post_g, ffn2_w_gu, ffn2_w_down):
    b, t, d = x.shape
    n = b * t
    for l in range(ffn1_pre_g.shape[0]):
        x = _ffn(x.reshape(n, d), ffn1_pre_g[l], ffn1_post_g[l], ffn1_w_gu[l], ffn1_w_down[l])
        p = _mixer_params(mix_pre_g[l], mix_post_g[l], w_in[l], q_norm_g[l], kv_norm_g[l], w_uq[l],
                          w_uk[l], w_uv[l], w_idx_q[l], idx_ln_g[l], idx_ln_b[l], w_dsa_o[l], w_dw[l],
                          b_dw[l], conv_ln_g[l], conv_ln_b[l], w_conv_out[l], mem_norm_g[l],
                          w_mem_kv[l], w_mem_o[l], w_out[l])
        cq, ckv, ckvt, kidx, widxt, ga, mbm = _mix_pre(x.reshape(b, t, d), mem, p)
        oa = _mix_dsa((cq, ckv, ckvt, kidx, widxt), p, t)
        x = _post_ffn(x, oa.reshape(n, -1), ga.reshape(n, d), mbm.reshape(n, d), p,
                      ffn2_pre_g[l], ffn2_post_g[l], ffn2_w_gu[l], ffn2_w_down[l])
        x = x.reshape(b, t, d)
    return x
```

```python
import functools
import math

import jax
import jax.numpy as jnp
from jax import lax
from jax.experimental import pallas as pl
from jax.experimental.pallas import tpu as pltpu

F32 = jnp.float32
BF16 = jnp.bfloat16
I32 = jnp.int32

EPS = 1e-6
NEG_INF = -1e30
D_MODEL = 1024
D_FF = 2816
N_HEADS_A = 8
D_QK = 64
D_V = 64
Q_LORA = 256
KV_LORA = 256
N_IDX = 4
D_IDX = 64
TOPK_MAX = 256
CONV_CH = 512
CONV_WIDTH = 31
N_HEADS_M = 4
D_HEAD_M = 128
N_MEM = 256

LANES = 128
SUBLANES = 8
BF16_ROWS = 16
VMEM_LIMIT = 56 * 1024 * 1024
INT_MIN = -2147483648
LOG2E = math.log2(math.e)

FFN_TM = 512
FFN_CHUNKS = ((0, 1536), (1536, 1280))
PRE_TM = 512
HALO = 32
TQ = 256
CK = 256
CNT_ROWS = 32


def _rms(x, g):
    return x * lax.rsqrt(jnp.mean(x * x, axis=-1, keepdims=True) + EPS) * g


def _layer_norm(x, g, b):
    mu = jnp.mean(x, axis=-1, keepdims=True)
    xc = x - mu
    var = jnp.mean(xc * xc, axis=-1, keepdims=True)
    return xc * lax.rsqrt(var + EPS) * g + b


def _dot(a, b):
    return jnp.dot(a, b, preferred_element_type=F32)


def _dot_nt(a, b):
    return lax.dot_general(a, b, (((1,), (1,)), ((), ())), preferred_element_type=F32)


def _const_spec(shape):
    nd = len(shape)
    return pl.BlockSpec(shape, lambda *_: (0,) * nd, pipeline_mode=pl.Buffered(1))


def _ffn_body(x, gpre_ref, gpost_ref, wg_ref, wu_ref, wd_ref):
    h = _rms(x, gpre_ref[...]).astype(BF16)
    acc = None
    for start, size in FFN_CHUNKS:
        g = _dot(h, wg_ref[:, start:start + size])
        u = _dot(h, wu_ref[:, start:start + size])
        a = (g * jax.nn.sigmoid(g) * u).astype(BF16)
        part = _dot(a, wd_ref[start:start + size, :])
        acc = part if acc is None else acc + part
    return x + 0.5 * _rms(acc, gpost_ref[...])


def _ffn_kernel(x_ref, gpre_ref, gpost_ref, wg_ref, wu_ref, wd_ref, o_ref):
    o_ref[...] = _ffn_body(x_ref[...], gpre_ref, gpost_ref, wg_ref, wu_ref, wd_ref)


def _post_ffn_kernel(x_ref, oa_ref, ga_ref, mbm_ref, wdo_ref, wout_ref, gmix_ref,
                     gpre_ref, gpost_ref, wg_ref, wu_ref, wd_ref, o_ref):
    y_a = _dot(oa_ref[...], wdo_ref[...])
    merged = ga_ref[...].astype(F32) * y_a + mbm_ref[...].astype(F32)
    x = x_ref[...] + _rms(_dot(merged.astype(BF16), wout_ref[...]), gmix_ref[...])
    o_ref[...] = _ffn_body(x, gpre_ref, gpost_ref, wg_ref, wu_ref, wd_ref)


def _ffn_weights(g_pre, g_post, w_gu, w_down):
    d = g_pre.shape[-1]
    return [g_pre.reshape(1, d), g_post.reshape(1, d), w_gu[:, :D_FF].astype(BF16),
            w_gu[:, D_FF:].astype(BF16), w_down.astype(BF16)]


def _ffn(x2d, g_pre, g_post, w_gu, w_down):
    n, d = x2d.shape
    assert n % FFN_TM == 0 and d == D_MODEL
    consts = _ffn_weights(g_pre, g_post, w_gu, w_down)
    row = pl.BlockSpec((FFN_TM, d), lambda i: (i, 0))
    return pl.pallas_call(
        _ffn_kernel,
        grid=(n // FFN_TM,),
        in_specs=[row] + [_const_spec(c.shape) for c in consts],
        out_specs=row,
        out_shape=jax.ShapeDtypeStruct((n, d), F32),
        compiler_params=pltpu.CompilerParams(
            dimension_semantics=("arbitrary",), vmem_limit_bytes=VMEM_LIMIT),
        name="ffn",
    )(x2d, *consts)


def _post_ffn(x2d, oa2d, ga2d, mbm2d, p, g_pre, g_post, w_gu, w_down):
    n, d = x2d.shape
    assert n % FFN_TM == 0 and d == D_MODEL
    consts = [p["w_dsa_o"], p["w_out"], p["mix_post_g"]] + _ffn_weights(g_pre, g_post, w_gu, w_down)

    def row(width):
        return pl.BlockSpec((FFN_TM, width), lambda i: (i, 0))

    return pl.pallas_call(
        _post_ffn_kernel,
        grid=(n // FFN_TM,),
        in_specs=[row(d), row(oa2d.shape[1]), row(d), row(d)] + [_const_spec(c.shape) for c in consts],
        out_specs=row(d),
        out_shape=jax.ShapeDtypeStruct((n, d), F32),
        compiler_params=pltpu.CompilerParams(
            dimension_semantics=("arbitrary",), vmem_limit_bytes=VMEM_LIMIT),
        name="post_ffn",
    )(x2d, oa2d, ga2d, mbm2d, *consts)


def _mix_pre_kernel(x_ref, mem_ref, gpre_ref, wcq_ref, wckv_ref, wsm_ref, wglu_ref, wqm_ref,
                    wgate_ref, qng_ref, kvng_ref, ilg_ref, ilb_ref, wdw_ref, bdw_ref, clg_ref,
                    clb_ref, wco_ref, mng_ref, wmkv_ref, wmo_ref,
                    cq_ref, ckv_ref, ckvt_ref, kidx_ref, widxt_ref, ga_ref, mbm_ref,
                    ubuf_ref, kmem_ref, vmem_ref):
    j = pl.program_id(1)
    tm = x_ref.shape[1]
    hm = N_HEADS_M * D_HEAD_M

    @pl.when(j == 0)
    def _():
        ubuf_ref[0:HALO, :] = jnp.zeros((HALO, CONV_CH), F32)
        mk = _rms(mem_ref[0], mng_ref[...]).astype(BF16)
        kv = _dot(mk, wmkv_ref[...])
        kmem_ref[...] = kv[:, :hm].astype(BF16)
        vmem_ref[...] = kv[:, hm:].astype(BF16)

    h = _rms(x_ref[0], gpre_ref[...]).astype(BF16)

    cq_ref[0] = _rms(_dot(h, wcq_ref[...]), qng_ref[...]).astype(BF16)
    ckv = _rms(_dot(h, wckv_ref[...]), kvng_ref[...])
    ckv_ref[0] = ckv.astype(BF16)
    for k in range(tm // CK):
        ckvt_ref[0, k, 0:KV_LORA, :] = ckv[k * CK:(k + 1) * CK, :].T.astype(BF16)
        ckvt_ref[0, k, KV_LORA:KV_LORA + BF16_ROWS, :] = jnp.ones((BF16_ROWS, CK), BF16)
    small = _dot(h, wsm_ref[...])
    kidx_ref[0] = _layer_norm(small[:, :D_IDX], ilg_ref[...], ilb_ref[...]).astype(BF16)
    small_t = small.T
    widxt_ref[0] = small_t[D_IDX:D_IDX + SUBLANES, :] * (N_IDX ** -0.5)

    uglu = _dot(h, wglu_ref[...])
    ubuf_ref[HALO:HALO + tm, :] = uglu[:, :CONV_CH] * jax.nn.sigmoid(uglu[:, CONV_CH:])
    first = HALO - (CONV_WIDTH - 1)
    y = bdw_ref[...] + wdw_ref[0:1, :] * ubuf_ref[first:first + tm, :]
    for t in range(1, CONV_WIDTH):
        y = y + wdw_ref[t:t + 1, :] * ubuf_ref[first + t:first + t + tm, :]
    ubuf_ref[0:HALO, :] = ubuf_ref[tm:tm + HALO, :]
    z = _layer_norm(y, clg_ref[...], clb_ref[...])
    y_b = _dot((z * jax.nn.sigmoid(z)).astype(BF16), wco_ref[...])

    qm = _dot(h, wqm_ref[...]).astype(BF16)
    heads = []
    for hh in range(N_HEADS_M):
        sl = slice(hh * D_HEAD_M, (hh + 1) * D_HEAD_M)
        s = _dot_nt(qm[:, sl], kmem_ref[:, sl]) * (D_HEAD_M ** -0.5)
        p = jnp.exp(s - jnp.max(s, axis=-1, keepdims=True))
        o = _dot(p.astype(BF16), vmem_ref[:, sl])
        heads.append((o / jnp.sum(p, axis=-1, keepdims=True)).astype(BF16))
    y_m = _dot(jnp.concatenate(heads, axis=-1), wmo_ref[...])

    gates = jax.nn.sigmoid(_dot(h, wgate_ref[...]))
    ga_ref[0] = gates[:, :D_MODEL].astype(BF16)
    mbm_ref[0] = (gates[:, D_MODEL:2 * D_MODEL] * y_b + gates[:, 2 * D_MODEL:] * y_m).astype(BF16)


def _mix_pre(x, mem, p):
    b, t, d = x.shape
    tm = min(PRE_TM, t)
    assert t % tm == 0 and tm % CK == 0 and d == D_MODEL and mem.shape == (b, N_MEM, d)
    hm = N_HEADS_M * D_HEAD_M

    def tile(width):
        return pl.BlockSpec((1, tm, width), lambda i, j: (i, j, 0))

    consts = [p["mix_pre_g"], p["w_cq"], p["w_ckv"], p["w_small"], p["w_glu"], p["w_qmem"],
              p["w_gate"], p["q_norm_g"], p["kv_norm_g"], p["idx_ln_g"], p["idx_ln_b"], p["w_dw"],
              p["b_dw"], p["conv_ln_g"], p["conv_ln_b"], p["w_conv_out"], p["mem_norm_g"],
              p["w_mem_kv"], p["w_mem_o"]]
    kvt_rows = KV_LORA + BF16_ROWS
    out_shape = (
        jax.ShapeDtypeStruct((b, t, Q_LORA), BF16),
        jax.ShapeDtypeStruct((b, t, KV_LORA), BF16),
        jax.ShapeDtypeStruct((b, t // CK, kvt_rows, CK), BF16),
        jax.ShapeDtypeStruct((b, t, D_IDX), BF16),
        jax.ShapeDtypeStruct((b, SUBLANES, t), F32),
        jax.ShapeDtypeStruct((b, t, d), BF16),
        jax.ShapeDtypeStruct((b, t, d), BF16),
    )
    out_specs = (
        tile(Q_LORA), tile(KV_LORA),
        pl.BlockSpec((1, tm // CK, kvt_rows, CK), lambda i, j: (i, j, 0, 0)),
        tile(D_IDX),
        pl.BlockSpec((1, SUBLANES, tm), lambda i, j: (i, 0, j)),
        tile(d), tile(d),
    )
    return pl.pallas_call(
        _mix_pre_kernel,
        grid=(b, t // tm),
        in_specs=[tile(d), pl.BlockSpec((1, N_MEM, d), lambda i, j: (i, 0, 0))]
        + [_const_spec(c.shape) for c in consts],
        out_specs=out_specs,
        out_shape=out_shape,
        scratch_shapes=[pltpu.VMEM((HALO + tm, CONV_CH), F32),
                        pltpu.VMEM((N_MEM, hm), BF16), pltpu.VMEM((N_MEM, hm), BF16)],
        compiler_params=pltpu.CompilerParams(
            dimension_semantics=("arbitrary", "arbitrary"), vmem_limit_bytes=VMEM_LIMIT),
        name="mix_pre",
    )(x, mem, *consts)


def _key_to_f32(key):
    return pltpu.bitcast(jnp.where(key < 0, key ^ 0x7FFFFFFF, key), F32)


def _mix_dsa_kernel(topk, cq_ref, widxt_ref, ckv_ref, ckvt_ref, kidx_ref,
                    wiqt_ref, wuqt_ref, wuk_ref, wuvt_ref, ltri_ref,
                    o_ref, sc_ref, acc_ref, qlat_ref, s_ref):
    qi = pl.program_id(1)
    nch = (qi * TQ + TQ + CK - 1) // CK

    cq = cq_ref[0]
    qpos = qi * TQ + lax.broadcasted_iota(I32, (CK, TQ), 1)
    krow = lax.broadcasted_iota(I32, (CK, TQ), 0)

    qidx_t = (_dot_nt(wiqt_ref[...], cq) * (D_IDX ** -0.5)).astype(BF16)
    widx_t = widxt_ref[0]

    def score_body(c, carry):
        start = pl.multiple_of(c * CK, CK)
        kc = kidx_ref[0, pl.ds(start, CK), :]
        score = None
        for hh in range(N_IDX):
            lg = _dot(kc, qidx_t[hh * D_IDX:(hh + 1) * D_IDX, :])
            term = widx_t[hh:hh + 1, :] * jnp.maximum(lg, 0.0)
            score = term if score is None else score + term
        sc_ref[pl.ds(start, CK), :] = jnp.where(krow + start <= qpos, score + 0.0, NEG_INF)
        return carry

    lax.fori_loop(0, nch, score_body, 0)

    def count(pred):
        def body(c, cnt):
            start = pl.multiple_of(c * CK, CK)
            hit = jnp.where(pred(sc_ref[pl.ds(start, CK), :]), 1, 0)
            return cnt + hit.reshape(CK // CNT_ROWS, CNT_ROWS, TQ).sum(axis=0)
        cnt = lax.fori_loop(0, nch, body, jnp.zeros((CNT_ROWS, TQ), I32))
        return jnp.sum(cnt, axis=0, keepdims=True)

    def max_below(bound):
        def body(c, best):
            start = pl.multiple_of(c * CK, CK)
            s = sc_ref[pl.ds(start, CK), :]
            cand = jnp.where(s < bound, s, -jnp.inf)
            return jnp.maximum(best, cand.reshape(CK // CNT_ROWS, CNT_ROWS, TQ).max(axis=0))
        best = lax.fori_loop(0, nch, body, jnp.full((CNT_ROWS, TQ), -jnp.inf, F32))
        return jnp.max(best, axis=0, keepdims=True)

    def bit_body(i, prefix):
        cand = prefix | (jnp.int32(1) << (31 - i))
        thr = _key_to_f32(cand ^ INT_MIN)
        return jnp.where(count(lambda s: s >= thr) >= topk, cand, prefix)

    lo_key = lax.fori_loop(0, 32, bit_body, jnp.zeros((1, TQ), I32)) ^ INT_MIN
    v1 = max_below(_key_to_f32(lo_key + 1))
    c1 = count(lambda s: s >= v1)

    def step_down():
        v2 = max_below(v1)
        return jnp.where(c1 >= topk, v1, v2), jnp.where(c1 >= topk, c1, count(lambda s: s >= v2))

    kth, n_ge = lax.cond(jnp.min(c1) < topk, step_down, lambda: (v1, c1))
    n_tie = topk - count(lambda s: s > kth)
    n_tie_f = n_tie.astype(F32)

    def mask_body(ranked, c, tie_seen):
        start = pl.multiple_of(c * CK, CK)
        sc = sc_ref[pl.ds(start, CK), :]
        if ranked:
            tie = sc == kth
            tie_rank = _dot(ltri_ref[...], jnp.where(tie, 1.0, 0.0).astype(BF16)) + tie_seen
            bias = jnp.where(sc > kth, 0.0,
                             jnp.where(tie, jnp.where(tie_rank <= n_tie_f, 0.0, NEG_INF), NEG_INF))
            tie_seen = tie_rank[CK - 1:CK, :]
        else:
            bias = jnp.where(sc >= kth, 0.0, NEG_INF)
        sc_ref[pl.ds(start, CK), :] = jnp.where(krow + start <= qpos, bias, NEG_INF)
        return tie_seen

    ranked = jnp.max(jnp.where(n_ge > topk, 1, 0)) > 0
    no_ties = jnp.zeros((1, TQ), F32)

    @pl.when(ranked)
    def _():
        lax.fori_loop(0, nch, functools.partial(mask_body, True), no_ties)

    @pl.when(jnp.logical_not(ranked))
    def _():
        lax.fori_loop(0, nch, functools.partial(mask_body, False), no_ties)

    q_t = _dot_nt(wuqt_ref[...], cq).astype(BF16)
    for hh in range(N_HEADS_A):
        qlat_ref[hh] = (_dot(wuk_ref[hh], q_t[hh * D_QK:(hh + 1) * D_QK, :])
                        * (D_QK ** -0.5 * LOG2E)).astype(BF16)

    acc_ref[...] = jnp.zeros_like(acc_ref)

    def attn_body(c, carry):
        m_run, l_run = carry
        start = pl.multiple_of(c * CK, CK)
        bias = sc_ref[pl.ds(start, CK), :]
        kc = ckv_ref[0, pl.ds(start, CK), :]
        kct = ckvt_ref[0, c]
        m_out, l_out = [], []
        for hh in range(N_HEADS_A):
            s = _dot(kc, qlat_ref[hh]) + bias
            s_ref[hh] = s
            m_out.append(jnp.maximum(m_run[hh], jnp.max(s, axis=0, keepdims=True)))
        for hh in range(N_HEADS_A):
            p = jnp.exp2(s_ref[hh] - jnp.maximum(m_out[hh], 0.1 * NEG_INF)).astype(BF16)
            alpha = jnp.exp2(m_run[hh] - m_out[hh])
            pv = _dot(kct, p)
            acc_ref[hh] = alpha * acc_ref[hh] + pv[0:KV_LORA, :]
            l_out.append(alpha * l_run[hh] + pv[KV_LORA:KV_LORA + 1, :])
        return tuple(m_out), tuple(l_out)

    init = (tuple(jnp.full((1, TQ), NEG_INF, F32) for _ in range(N_HEADS_A)),
            tuple(jnp.zeros((1, TQ), F32) for _ in range(N_HEADS_A)))
    _, l_fin = lax.fori_loop(0, nch, attn_body, init)

    o_t = jnp.concatenate(
        [_dot(wuvt_ref[hh], (acc_ref[hh] * (1.0 / l_fin[hh])).astype(BF16)) for hh in range(N_HEADS_A)],
        axis=0)
    o_ref[0] = o_t.T.astype(BF16)


def _mix_dsa(pre, p, t):
    cq, ckv, ckvt, kidx, widxt = pre
    b = cq.shape[0]
    assert t % CK == 0 and t % TQ == 0
    topk = min(TOPK_MAX, t // 4)
    assert topk <= CK
    ltri = (lax.broadcasted_iota(I32, (CK, CK), 1) <= lax.broadcasted_iota(I32, (CK, CK), 0)).astype(BF16)
    consts = [p["w_idx_q_t"], p["w_uq_t"], p["w_uk_h"], p["w_uv_t"], ltri]

    def whole(shape):
        nd = len(shape)
        return pl.BlockSpec((1,) + tuple(shape[1:]), lambda i, j: (i,) + (0,) * (nd - 1))

    return pl.pallas_call(
        functools.partial(_mix_dsa_kernel, topk),
        grid=(b, t // TQ),
        in_specs=[pl.BlockSpec((1, TQ, Q_LORA), lambda i, j: (i, j, 0)),
                  pl.BlockSpec((1, SUBLANES, TQ), lambda i, j: (i, 0, j)),
                  whole(ckv.shape), whole(ckvt.shape), whole(kidx.shape)]
        + [_const_spec(c.shape) for c in consts],
        out_specs=pl.BlockSpec((1, TQ, N_HEADS_A * D_V), lambda i, j: (i, j, 0)),
        out_shape=jax.ShapeDtypeStruct((b, t, N_HEADS_A * D_V), BF16),
        scratch_shapes=[pltpu.VMEM((t, TQ), F32), pltpu.VMEM((N_HEADS_A, KV_LORA, TQ), F32),
                        pltpu.VMEM((N_HEADS_A, KV_LORA, TQ), BF16), pltpu.VMEM((N_HEADS_A, CK, TQ), F32)],
        compiler_params=pltpu.CompilerParams(
            dimension_semantics=("arbitrary", "arbitrary"), vmem_limit_bytes=VMEM_LIMIT),
        name="mix_dsa",
    )(cq, widxt, ckv, ckvt, kidx, *consts)


def _mixer_params(mix_pre_g, mix_post_g, w_in, q_norm_g, kv_norm_g, w_uq, w_uk, w_uv, w_idx_q,
                  idx_ln_g, idx_ln_b, w_dsa_o, w_dw, b_dw, conv_ln_g, conv_ln_b, w_conv_out,
                  mem_norm_g, w_mem_kv, w_mem_o, w_out):
    hm = N_HEADS_M * D_HEAD_M
    cuts = [0]
    for width in (Q_LORA, KV_LORA, D_IDX, N_IDX, 2 * CONV_CH, hm, 3 * D_MODEL):
        cuts.append(cuts[-1] + width)
    assert w_in.shape == (D_MODEL, cuts[-1])
    w_small = jnp.pad(w_in[:, cuts[2]:cuts[4]], ((0, 0), (0, LANES - D_IDX - N_IDX)))
    row = lambda v: v.reshape(1, -1).astype(F32)
    return {
        "mix_pre_g": row(mix_pre_g), "mix_post_g": row(mix_post_g),
        "w_cq": w_in[:, cuts[0]:cuts[1]].astype(BF16), "w_ckv": w_in[:, cuts[1]:cuts[2]].astype(BF16),
        "w_small": w_small.astype(BF16),
        "w_glu": w_in[:, cuts[4]:cuts[5]].astype(BF16), "w_qmem": w_in[:, cuts[5]:cuts[6]].astype(BF16),
        "w_gate": w_in[:, cuts[6]:cuts[7]].astype(BF16),
        "q_norm_g": row(q_norm_g), "kv_norm_g": row(kv_norm_g),
        "idx_ln_g": row(idx_ln_g), "idx_ln_b": row(idx_ln_b),
        "w_dw": w_dw.astype(F32), "b_dw": row(b_dw),
        "conv_ln_g": row(conv_ln_g), "conv_ln_b": row(conv_ln_b),
        "w_conv_out": w_conv_out.astype(BF16), "mem_norm_g": row(mem_norm_g),
        "w_mem_kv": w_mem_kv.astype(BF16), "w_mem_o": w_mem_o.astype(BF16),
        "w_idx_q_t": w_idx_q.T.astype(BF16), "w_uq_t": w_uq.T.astype(BF16),
        "w_uk_h": jnp.transpose(w_uk, (1, 0, 2)).astype(BF16),
        "w_uv_t": jnp.transpose(w_uv, (1, 2, 0)).astype(BF16),
        "w_dsa_o": w_dsa_o.astype(BF16), "w_out": w_out.astype(BF16),
    }


def kernel(x, mem, ffn1_pre_g, ffn1_post_g, ffn1_w_gu, ffn1_w_down, mix_pre_g, mix_post_g, w_in, q_norm_g, kv_norm_g, w_uq, w_uk, w_uv, w_idx_q, idx_ln_g, idx_ln_b, w_dsa_o, w_dw, b_dw, conv_ln_g, conv_ln_b, w_conv_out, mem_norm_g, w_mem_kv, w_mem_o, w_out, ffn2_pre_g, ffn2_post_g, ffn2_w_gu, ffn2_w_down):
    b, t, d = x.shape
    n = b * t
    for l in range(ffn1_pre_g.shape[0]):
        x = _ffn(x.reshape(n, d), ffn1_pre_g[l], ffn1_post_g[l], ffn1_w_gu[l], ffn1_w_down[l])
        p = _mixer_params(mix_pre_g[l], mix_post_g[l], w_in[l], q_norm_g[l], kv_norm_g[l], w_uq[l],
                          w_uk[l], w_uv[l], w_idx_q[l], idx_ln_g[l], idx_ln_b[l], w_dsa_o[l], w_dw[l],
                          b_dw[l], conv_ln_g[l], conv_ln_b[l], w_conv_out[l], mem_norm_g[l],
                          w_mem_kv[l], w_mem_o[l], w_out[l])
        cq, ckv, ckvt, kidx, widxt, ga, mbm = _mix_pre(x.reshape(b, t, d), mem, p)
        oa = _mix_dsa((cq, ckv, ckvt, kidx, widxt), p, t)
        x = _post_ffn(x, oa.reshape(n, -1), ga.reshape(n, d), mbm.reshape(n, d), p,
                      ffn2_pre_g[l], ffn2_post_g[l], ffn2_w_gu[l], ffn2_w_down[l])
        x = x.reshape(b, t, d)
    return x
```

```python
import functools
import math

import jax
import jax.numpy as jnp
from jax import lax
from jax.experimental import pallas as pl
from jax.experimental.pallas import tpu as pltpu

F32 = jnp.float32
BF16 = jnp.bfloat16
I32 = jnp.int32

EPS = 1e-6
NEG_INF = -1e30
D_MODEL = 1024
D_FF = 2816
N_HEADS_A = 8
D_QK = 64
D_V = 64
Q_LORA = 256
KV_LORA = 256
N_IDX = 4
D_IDX = 64
TOPK_MAX = 256
CONV_CH = 512
CONV_WIDTH = 31
N_HEADS_M = 4
D_HEAD_M = 128
N_MEM = 256

LANES = 128
SUBLANES = 8
BF16_ROWS = 16
VMEM_LIMIT = 56 * 1024 * 1024
F32_BIG = 3.0e38
SEARCH_CAP = 288
SEARCH_UNCHECKED = 18
LOG2E = math.log2(math.e)

FFN_TM = 512
FFN_CHUNKS = ((0, 1536), (1536, 1280))
PRE_TM = 512
HALO = 32
GATE_COLS = 512
TQ = 256
CK = 256
CNT_ROWS = 32


def _rms(x, g):
    return x * lax.rsqrt(jnp.mean(x * x, axis=-1, keepdims=True) + EPS) * g


def _layer_norm(x, g, b):
    mu = jnp.mean(x, axis=-1, keepdims=True)
    xc = x - mu
    var = jnp.mean(xc * xc, axis=-1, keepdims=True)
    return xc * lax.rsqrt(var + EPS) * g + b


def _dot(a, b):
    return jnp.dot(a, b, preferred_element_type=F32)


def _dot_nt(a, b):
    return lax.dot_general(a, b, (((1,), (1,)), ((), ())), preferred_element_type=F32)


def _const_spec(shape):
    nd = len(shape)
    return pl.BlockSpec(shape, lambda *_: (0,) * nd, pipeline_mode=pl.Buffered(1))


def _ffn_body(x, gpre_ref, gpost_ref, wg_ref, wu_ref, wd_ref):
    h = _rms(x, gpre_ref[...]).astype(BF16)
    acc = None
    for start, size in FFN_CHUNKS:
        g = _dot(h, wg_ref[:, start:start + size])
        u = _dot(h, wu_ref[:, start:start + size])
        a = (g * jax.nn.sigmoid(g) * u).astype(BF16)
        part = _dot(a, wd_ref[start:start + size, :])
        acc = part if acc is None else acc + part
    return x + 0.5 * _rms(acc, gpost_ref[...])


def _ffn_kernel(x_ref, gpre_ref, gpost_ref, wg_ref, wu_ref, wd_ref, o_ref):
    o_ref[...] = _ffn_body(x_ref[...], gpre_ref, gpost_ref, wg_ref, wu_ref, wd_ref)


def _post_ffn_kernel(x_ref, oa_ref, ga_ref, mbm_ref, wdo_ref, wout_ref, gmix_ref,
                     gpre_ref, gpost_ref, wg_ref, wu_ref, wd_ref, o_ref):
    y_a = _dot(oa_ref[...], wdo_ref[...])
    merged = ga_ref[...].astype(F32) * y_a + mbm_ref[...].astype(F32)
    x = x_ref[...] + _rms(_dot(merged.astype(BF16), wout_ref[...]), gmix_ref[...])
    o_ref[...] = _ffn_body(x, gpre_ref, gpost_ref, wg_ref, wu_ref, wd_ref)


def _ffn_weights(g_pre, g_post, w_gu, w_down):
    d = g_pre.shape[-1]
    return [g_pre.reshape(1, d), g_post.reshape(1, d), w_gu[:, :D_FF].astype(BF16),
            w_gu[:, D_FF:].astype(BF16), w_down.astype(BF16)]


def _ffn(x2d, g_pre, g_post, w_gu, w_down):
    n, d = x2d.shape
    assert n % FFN_TM == 0 and d == D_MODEL
    consts = _ffn_weights(g_pre, g_post, w_gu, w_down)
    row = pl.BlockSpec((FFN_TM, d), lambda i: (i, 0))
    return pl.pallas_call(
        _ffn_kernel,
        grid=(n // FFN_TM,),
        in_specs=[row] + [_const_spec(c.shape) for c in consts],
        out_specs=row,
        out_shape=jax.ShapeDtypeStruct((n, d), F32),
        compiler_params=pltpu.CompilerParams(
            dimension_semantics=("arbitrary",), vmem_limit_bytes=VMEM_LIMIT),
        name="ffn",
    )(x2d, *consts)


def _post_ffn(x2d, oa2d, ga2d, mbm2d, p, g_pre, g_post, w_gu, w_down):
    n, d = x2d.shape
    assert n % FFN_TM == 0 and d == D_MODEL
    consts = [p["w_dsa_o"], p["w_out"], p["mix_post_g"]] + _ffn_weights(g_pre, g_post, w_gu, w_down)

    def row(width):
        return pl.BlockSpec((FFN_TM, width), lambda i: (i, 0))

    return pl.pallas_call(
        _post_ffn_kernel,
        grid=(n // FFN_TM,),
        in_specs=[row(d), row(oa2d.shape[1]), row(d), row(d)] + [_const_spec(c.shape) for c in consts],
        out_specs=row(d),
        out_shape=jax.ShapeDtypeStruct((n, d), F32),
        compiler_params=pltpu.CompilerParams(
            dimension_semantics=("arbitrary",), vmem_limit_bytes=VMEM_LIMIT),
        name="post_ffn",
    )(x2d, oa2d, ga2d, mbm2d, *consts)


def _mix_pre_kernel(x_ref, mem_ref, gpre_ref, wcq_ref, wckv_ref, wsm_ref, wglu_ref, wqm_ref,
                    wgate_ref, qng_ref, kvng_ref, ilg_ref, ilb_ref, wdw_ref, bdw_ref, clg_ref,
                    clb_ref, wco_ref, mng_ref, wmkv_ref, wmo_ref,
                    cq_ref, ckv_ref, ckvt_ref, kidx_ref, widxt_ref, ga_ref, mbm_ref,
                    ubuf_ref, ushift_ref, kmem_ref, vmem_ref):
    j = pl.program_id(1)
    tm = x_ref.shape[1]
    hm = N_HEADS_M * D_HEAD_M

    @pl.when(j == 0)
    def _():
        ubuf_ref[0:HALO, :] = jnp.zeros((HALO, CONV_CH), F32)
        mk = _rms(mem_ref[0], mng_ref[...]).astype(BF16)
        kv = _dot(mk, wmkv_ref[...])
        kmem_ref[...] = kv[:, :hm].astype(BF16)
        vmem_ref[...] = kv[:, hm:].astype(BF16)

    h = _rms(x_ref[0], gpre_ref[...]).astype(BF16)

    uglu = _dot(h, wglu_ref[...])
    ubuf_ref[HALO:HALO + tm, :] = uglu[:, :CONV_CH] * jax.nn.sigmoid(uglu[:, CONV_CH:])
    span = tm + HALO - SUBLANES
    for r in range(1, SUBLANES):
        ushift_ref[r - 1] = ubuf_ref[r:r + span, :]

    first = HALO - (CONV_WIDTH - 1)
    y = bdw_ref[...]
    for t in range(CONV_WIDTH):
        shift, base = (first + t) % SUBLANES, (first + t) // SUBLANES * SUBLANES
        rows = ubuf_ref[base:base + tm, :] if shift == 0 else ushift_ref[shift - 1, base:base + tm, :]
        y = y + wdw_ref[t:t + 1, :] * rows
    ubuf_ref[0:HALO, :] = ubuf_ref[tm:tm + HALO, :]
    z = _layer_norm(y, clg_ref[...], clb_ref[...])
    y_b = _dot((z * jax.nn.sigmoid(z)).astype(BF16), wco_ref[...])

    cq_ref[0] = _rms(_dot(h, wcq_ref[...]), qng_ref[...]).astype(BF16)
    ckv = _rms(_dot(h, wckv_ref[...]), kvng_ref[...])
    ckv_ref[0] = ckv.astype(BF16)
    for k in range(tm // CK):
        ckvt_ref[0, k, 0:KV_LORA, :] = ckv[k * CK:(k + 1) * CK, :].T.astype(BF16)
        ckvt_ref[0, k, KV_LORA:KV_LORA + BF16_ROWS, :] = jnp.ones((BF16_ROWS, CK), BF16)
    small = _dot(h, wsm_ref[...])
    kidx_ref[0] = _layer_norm(small[:, :D_IDX], ilg_ref[...], ilb_ref[...]).astype(BF16)
    small_t = small.T
    widxt_ref[0] = small_t[D_IDX:D_IDX + SUBLANES, :] * (N_IDX ** -0.5)

    qm = _dot(h, wqm_ref[...]).astype(BF16)
    heads = []
    for hh in range(N_HEADS_M):
        sl = slice(hh * D_HEAD_M, (hh + 1) * D_HEAD_M)
        s = _dot_nt(qm[:, sl], kmem_ref[:, sl]) * (D_HEAD_M ** -0.5)
        p = jnp.exp(s - jnp.max(s, axis=-1, keepdims=True))
        o = _dot(p.astype(BF16), vmem_ref[:, sl])
        heads.append((o / jnp.sum(p, axis=-1, keepdims=True)).astype(BF16))
    y_m = _dot(jnp.concatenate(heads, axis=-1), wmo_ref[...])

    for c0 in range(0, D_MODEL, GATE_COLS):
        def gate(branch):
            return jax.nn.sigmoid(_dot(h, wgate_ref[:, branch * D_MODEL + c0:branch * D_MODEL + c0 + GATE_COLS]))
        ga_ref[0, :, c0:c0 + GATE_COLS] = gate(0).astype(BF16)
        mbm_ref[0, :, c0:c0 + GATE_COLS] = (gate(1) * y_b[:, c0:c0 + GATE_COLS]
                                            + gate(2) * y_m[:, c0:c0 + GATE_COLS]).astype(BF16)


def _mix_pre(x, mem, p):
    b, t, d = x.shape
    tm = min(PRE_TM, t)
    assert t % tm == 0 and tm % CK == 0 and d == D_MODEL and mem.shape == (b, N_MEM, d)
    hm = N_HEADS_M * D_HEAD_M

    def tile(width):
        return pl.BlockSpec((1, tm, width), lambda i, j: (i, j, 0))

    consts = [p["mix_pre_g"], p["w_cq"], p["w_ckv"], p["w_small"], p["w_glu"], p["w_qmem"],
              p["w_gate"], p["q_norm_g"], p["kv_norm_g"], p["idx_ln_g"], p["idx_ln_b"], p["w_dw"],
              p["b_dw"], p["conv_ln_g"], p["conv_ln_b"], p["w_conv_out"], p["mem_norm_g"],
              p["w_mem_kv"], p["w_mem_o"]]
    kvt_rows = KV_LORA + BF16_ROWS
    out_shape = (
        jax.ShapeDtypeStruct((b, t, Q_LORA), BF16),
        jax.ShapeDtypeStruct((b, t, KV_LORA), BF16),
        jax.ShapeDtypeStruct((b, t // CK, kvt_rows, CK), BF16),
        jax.ShapeDtypeStruct((b, t, D_IDX), BF16),
        jax.ShapeDtypeStruct((b, SUBLANES, t), F32),
        jax.ShapeDtypeStruct((b, t, d), BF16),
        jax.ShapeDtypeStruct((b, t, d), BF16),
    )
    out_specs = (
        tile(Q_LORA), tile(KV_LORA),
        pl.BlockSpec((1, tm // CK, kvt_rows, CK), lambda i, j: (i, j, 0, 0)),
        tile(D_IDX),
        pl.BlockSpec((1, SUBLANES, tm), lambda i, j: (i, 0, j)),
        tile(d), tile(d),
    )
    return pl.pallas_call(
        _mix_pre_kernel,
        grid=(b, t // tm),
        in_specs=[tile(d), pl.BlockSpec((1, N_MEM, d), lambda i, j: (i, 0, 0))]
        + [_const_spec(c.shape) for c in consts],
        out_specs=out_specs,
        out_shape=out_shape,
        scratch_shapes=[pltpu.VMEM((HALO + tm, CONV_CH), F32),
                        pltpu.VMEM((SUBLANES - 1, HALO + tm - SUBLANES, CONV_CH), F32),
                        pltpu.VMEM((N_MEM, hm), BF16), pltpu.VMEM((N_MEM, hm), BF16)],
        compiler_params=pltpu.CompilerParams(
            dimension_semantics=("arbitrary", "arbitrary"), vmem_limit_bytes=VMEM_LIMIT),
        name="mix_pre",
    )(x, mem, *consts)


def _mix_dsa_kernel(topk, cq_ref, widxt_ref, ckv_ref, ckvt_ref, kidx_ref,
                    wiqt_ref, wuqt_ref, wuk_ref, wuvt_ref, ltri_ref,
                    o_ref, sc_ref, acc_ref, qlat_ref, s_ref):
    qi = pl.program_id(1)
    nch = (qi * TQ + TQ + CK - 1) // CK

    cq = cq_ref[0]
    qpos = qi * TQ + lax.broadcasted_iota(I32, (CK, TQ), 1)
    krow = lax.broadcasted_iota(I32, (CK, TQ), 0)

    qidx_t = (_dot_nt(wiqt_ref[...], cq) * (D_IDX ** -0.5)).astype(BF16)
    widx_t = widxt_ref[0]

    def fold(x, op):
        return op(x.reshape(CK // CNT_ROWS, CNT_ROWS, TQ), axis=0)

    def score_body(c, carry):
        n_ge0, n_gt0, s_max, s_min = carry
        start = pl.multiple_of(c * CK, CK)
        kc = kidx_ref[0, pl.ds(start, CK), :]
        score = None
        for hh in range(N_IDX):
            lg = _dot(kc, qidx_t[hh * D_IDX:(hh + 1) * D_IDX, :])
            term = widx_t[hh:hh + 1, :] * jnp.maximum(lg, 0.0)
            score = term if score is None else score + term
        causal = krow + start <= qpos
        score = jnp.where(causal, score + 0.0, NEG_INF)
        sc_ref[pl.ds(start, CK), :] = score
        return (n_ge0 + fold(jnp.where(score >= 0.0, 1, 0), jnp.sum),
                n_gt0 + fold(jnp.where(score > 0.0, 1, 0), jnp.sum),
                jnp.maximum(s_max, fold(score, jnp.max)),
                jnp.minimum(s_min, fold(jnp.where(causal, score, jnp.inf), jnp.min)))

    zeros_i = jnp.zeros((CNT_ROWS, TQ), I32)
    stats = lax.fori_loop(0, nch, score_body,
                          (zeros_i, zeros_i, jnp.full((CNT_ROWS, TQ), -jnp.inf, F32),
                           jnp.full((CNT_ROWS, TQ), jnp.inf, F32)))
    n_ge0 = jnp.sum(stats[0], axis=0, keepdims=True)
    n_gt0 = jnp.sum(stats[1], axis=0, keepdims=True)
    s_max = jnp.max(stats[2], axis=0, keepdims=True)
    s_min = jnp.min(stats[3], axis=0, keepdims=True)
    n_causal = qpos[0:1, :] + 1

    def count_ge(thr):
        def body(c, cnt):
            start = pl.multiple_of(c * CK, CK)
            return cnt + fold(jnp.where(sc_ref[pl.ds(start, CK), :] >= thr, 1, 0), jnp.sum)
        return jnp.sum(lax.fori_loop(0, nch, body, zeros_i), axis=0, keepdims=True)

    few = n_causal < topk
    zero_tie = jnp.where(n_gt0 < topk, jnp.where(n_ge0 >= topk, 1, 0), 0) == 1
    positive = n_gt0 >= topk
    lo = jnp.where(few, NEG_INF, jnp.where(positive, 0.0, jnp.where(zero_tie, 0.0, s_min)))
    hi = jnp.where(positive, jnp.minimum(s_max * 2.0 + 1.0, F32_BIG), 0.0)
    c_lo = jnp.where(few, topk, jnp.where(positive, n_ge0, jnp.where(zero_tie, n_ge0, n_causal)))
    c_hi = jnp.where(positive, 0, jnp.where(zero_tie, n_gt0, n_ge0))
    done = jnp.where(few, 1, jnp.where(zero_tie, 1, jnp.where(c_lo == topk, 1, 0)))

    def halve(state):
        lo, hi, c_lo, c_hi, done = state
        mid = lo + (hi - lo) * 0.5
        stuck = jnp.where(mid <= lo, 1, jnp.where(mid >= hi, 1, 0))
        c = count_ge(mid)
        live = jnp.where(done == 0, 1 - stuck, 0)
        raise_lo = jnp.where(c >= topk, live, 0) == 1
        lower_hi = jnp.where(c >= topk, 0, live) == 1
        lo, c_lo = jnp.where(raise_lo, mid, lo), jnp.where(raise_lo, c, c_lo)
        hi, c_hi = jnp.where(lower_hi, mid, hi), jnp.where(lower_hi, c, c_hi)
        return lo, hi, c_lo, c_hi, jnp.where(c_lo == topk, 1, jnp.maximum(done, stuck))

    state = lax.fori_loop(0, SEARCH_UNCHECKED, lambda _, st: halve(st), (lo, hi, c_lo, c_hi, done))

    def checked_halve(carry):
        state = halve(carry[2])
        return carry[0] + 1, jnp.min(state[4]), state

    _, _, (kth, _, n_ge, c_hi, _) = lax.while_loop(
        lambda carry: jnp.logical_and(carry[0] < SEARCH_CAP, carry[1] == 0), checked_halve,
        (jnp.int32(0), jnp.min(state[4]), state))
    n_tie_f = jnp.where(n_ge == topk, topk, topk - c_hi).astype(F32)

    def mask_body(ranked, c, tie_seen):
        start = pl.multiple_of(c * CK, CK)
        sc = sc_ref[pl.ds(start, CK), :]
        if ranked:
            tie = sc == kth
            tie_rank = _dot(ltri_ref[...], jnp.where(tie, 1.0, 0.0).astype(BF16)) + tie_seen
            bias = jnp.where(sc > kth, 0.0,
                             jnp.where(tie, jnp.where(tie_rank <= n_tie_f, 0.0, NEG_INF), NEG_INF))
            tie_seen = tie_rank[CK - 1:CK, :]
        else:
            bias = jnp.where(sc >= kth, 0.0, NEG_INF)
        sc_ref[pl.ds(start, CK), :] = jnp.where(krow + start <= qpos, bias, NEG_INF)
        return tie_seen

    ranked = jnp.max(jnp.where(n_ge > topk, 1, 0)) > 0
    no_ties = jnp.zeros((1, TQ), F32)

    @pl.when(ranked)
    def _():
        lax.fori_loop(0, nch, functools.partial(mask_body, True), no_ties)

    @pl.when(jnp.logical_not(ranked))
    def _():
        lax.fori_loop(0, nch, functools.partial(mask_body, False), no_ties)

    q_t = _dot_nt(wuqt_ref[...], cq).astype(BF16)
    for hh in range(N_HEADS_A):
        qlat_ref[hh] = (_dot(wuk_ref[hh], q_t[hh * D_QK:(hh + 1) * D_QK, :])
                        * (D_QK ** -0.5 * LOG2E)).astype(BF16)

    acc_ref[...] = jnp.zeros_like(acc_ref)

    def attn_body(c, carry):
        m_run, l_run = carry
        start = pl.multiple_of(c * CK, CK)
        bias = sc_ref[pl.ds(start, CK), :]
        kc = ckv_ref[0, pl.ds(start, CK), :]
        kct = ckvt_ref[0, c]
        m_out, l_out = [], []
        for hh in range(N_HEADS_A):
            s = _dot(kc, qlat_ref[hh]) + bias
            s_ref[hh] = s
            m_out.append(jnp.maximum(m_run[hh], jnp.max(s, axis=0, keepdims=True)))
        for hh in range(N_HEADS_A):
            p = jnp.exp2(s_ref[hh] - jnp.maximum(m_out[hh], 0.1 * NEG_INF)).astype(BF16)
            alpha = jnp.exp2(m_run[hh] - m_out[hh])
            pv = _dot(kct, p)
            acc_ref[hh] = alpha * acc_ref[hh] + pv[0:KV_LORA, :]
            l_out.append(alpha * l_run[hh] + pv[KV_LORA:KV_LORA + 1, :])
        return tuple(m_out), tuple(l_out)

    init = (tuple(jnp.full((1, TQ), NEG_INF, F32) for _ in range(N_HEADS_A)),
            tuple(jnp.zeros((1, TQ), F32) for _ in range(N_HEADS_A)))
    _, l_fin = lax.fori_loop(0, nch, attn_body, init)

    o_t = jnp.concatenate(
        [_dot(wuvt_ref[hh], (acc_ref[hh] * (1.0 / l_fin[hh])).astype(BF16)) for hh in range(N_HEADS_A)],
        axis=0)
    o_ref[0] = o_t.T.astype(BF16)


def _mix_dsa(pre, p, t):
    cq, ckv, ckvt, kidx, widxt = pre
    b = cq.shape[0]
    assert t % CK == 0 and t % TQ == 0
    topk = min(TOPK_MAX, t // 4)
    assert topk <= CK
    ltri = (lax.broadcasted_iota(I32, (CK, CK), 1) <= lax.broadcasted_iota(I32, (CK, CK), 0)).astype(BF16)
    consts = [p["w_idx_q_t"], p["w_uq_t"], p["w_uk_h"], p["w_uv_t"], ltri]

    def whole(shape):
        nd = len(shape)
        return pl.BlockSpec((1,) + tuple(shape[1:]), lambda i, j: (i,) + (0,) * (nd - 1))

    return pl.pallas_call(
        functools.partial(_mix_dsa_kernel, topk),
        grid=(b, t // TQ),
        in_specs=[pl.BlockSpec((1, TQ, Q_LORA), lambda i, j: (i, j, 0)),
                  pl.BlockSpec((1, SUBLANES, TQ), lambda i, j: (i, 0, j)),
                  whole(ckv.shape), whole(ckvt.shape), whole(kidx.shape)]
        + [_const_spec(c.shape) for c in consts],
        out_specs=pl.BlockSpec((1, TQ, N_HEADS_A * D_V), lambda i, j: (i, j, 0)),
        out_shape=jax.ShapeDtypeStruct((b, t, N_HEADS_A * D_V), BF16),
        scratch_shapes=[pltpu.VMEM((t, TQ), F32), pltpu.VMEM((N_HEADS_A, KV_LORA, TQ), F32),
                        pltpu.VMEM((N_HEADS_A, KV_LORA, TQ), BF16), pltpu.VMEM((N_HEADS_A, CK, TQ), F32)],
        compiler_params=pltpu.CompilerParams(
            dimension_semantics=("arbitrary", "arbitrary"), vmem_limit_bytes=VMEM_LIMIT),
        name="mix_dsa",
    )(cq, widxt, ckv, ckvt, kidx, *consts)


def _mixer_params(mix_pre_g, mix_post_g, w_in, q_norm_g, kv_norm_g, w_uq, w_uk, w_uv, w_idx_q,
                  idx_ln_g, idx_ln_b, w_dsa_o, w_dw, b_dw, conv_ln_g, conv_ln_b, w_conv_out,
                  mem_norm_g, w_mem_kv, w_mem_o, w_out):
    hm = N_HEADS_M * D_HEAD_M
    cuts = [0]
    for width in (Q_LORA, KV_LORA, D_IDX, N_IDX, 2 * CONV_CH, hm, 3 * D_MODEL):
        cuts.append(cuts[-1] + width)
    assert w_in.shape == (D_MODEL, cuts[-1])
    w_small = jnp.pad(w_in[:, cuts[2]:cuts[4]], ((0, 0), (0, LANES - D_IDX - N_IDX)))
    row = lambda v: v.reshape(1, -1).astype(F32)
    return {
        "mix_pre_g": row(mix_pre_g), "mix_post_g": row(mix_post_g),
        "w_cq": w_in[:, cuts[0]:cuts[1]].astype(BF16), "w_ckv": w_in[:, cuts[1]:cuts[2]].astype(BF16),
        "w_small": w_small.astype(BF16),
        "w_glu": w_in[:, cuts[4]:cuts[5]].astype(BF16), "w_qmem": w_in[:, cuts[5]:cuts[6]].astype(BF16),
        "w_gate": w_in[:, cuts[6]:cuts[7]].astype(BF16),
        "q_norm_g": row(q_norm_g), "kv_norm_g": row(kv_norm_g),
        "idx_ln_g": row(idx_ln_g), "idx_ln_b": row(idx_ln_b),
        "w_dw": w_dw.astype(F32), "b_dw": row(b_dw),
        "conv_ln_g": row(conv_ln_g), "conv_ln_b": row(conv_ln_b),
        "w_conv_out": w_conv_out.astype(BF16), "mem_norm_g": row(mem_norm_g),
        "w_mem_kv": w_mem_kv.astype(BF16), "w_mem_o": w_mem_o.astype(BF16),
        "w_idx_q_t": w_idx_q.T.astype(BF16), "w_uq_t": w_uq.T.astype(BF16),
        "w_uk_h": jnp.transpose(w_uk, (1, 0, 2)).astype(BF16),
        "w_uv_t": jnp.transpose(w_uv, (1, 2, 0)).astype(BF16),
        "w_dsa_o": w_dsa_o.astype(BF16), "w_out": w_out.astype(BF16),
    }


def kernel(x, mem, ffn1_pre_g, ffn1_post_g, ffn1_w_gu, ffn1_w_down, mix_pre_g, mix_post_g, w_in, q_norm_g, kv_norm_g, w_uq, w_uk, w_uv, w_idx_q, idx_ln_g, idx_ln_b, w_dsa_o, w_dw, b_dw, conv_ln_g, conv_ln_b, w_conv_out, mem_norm_g, w_mem_kv, w_mem_o, w_out, ffn2_pre_g, ffn2_post_g, ffn2_w_gu, ffn2_w_down):
    b, t, d = x.shape
    n = b * t
    for l in range(ffn1_pre_g.shape[0]):
        x = _ffn(x.reshape(n, d), ffn1_pre_g[l], ffn1_post_g[l], ffn1_w_gu[l], ffn1_w_down[l])
        p = _mixer_params(mix_pre_g[l], mix_post_g[l], w_in[l], q_norm_g[l], kv_norm_g[l], w_uq[l],
                          w_uk[l], w_uv[l], w_idx_q[l], idx_ln_g[l], idx_ln_b[l], w_dsa_o[l], w_dw[l],
                          b_dw[l], conv_ln_g[l], conv_ln_b[l], w_conv_out[l], mem_norm_g[l],
                          w_mem_kv[l], w_mem_o[l], w_out[l])
        cq, ckv, ckvt, kidx, widxt, ga, mbm = _mix_pre(x.reshape(b, t, d), mem, p)
        oa = _mix_dsa((cq, ckv, ckvt, kidx, widxt), p, t)
        x = _post_ffn(x, oa.reshape(n, -1), ga.reshape(n, d), mbm.reshape(n, d), p,
                      ffn2_pre_g[l], ffn2_post_g[l], ffn2_w_gu[l], ffn2_w_down[l])
        x = x.reshape(b, t, d)
    return x
```

```python
import functools
import math

import jax
import jax.numpy as jnp
from jax import lax
from jax.experimental import pallas as pl
from jax.experimental.pallas import tpu as pltpu

F32 = jnp.float32
BF16 = jnp.bfloat16
I32 = jnp.int32

EPS = 1e-6
NEG_INF = -1e30
D_MODEL = 1024
D_FF = 2816
N_HEADS_A = 8
D_QK = 64
D_V = 64
Q_LORA = 256
KV_LORA = 256
N_IDX = 4
D_IDX = 64
TOPK_MAX = 256
CONV_CH = 512
CONV_WIDTH = 31
N_HEADS_M = 4
D_HEAD_M = 128
N_MEM = 256

LANES = 128
SUBLANES = 8
BF16_ROWS = 16
VMEM_LIMIT = 56 * 1024 * 1024
F32_BIG = 3.0e38
SEARCH_CAP = 288
SEARCH_UNCHECKED = 19
LOG2E = math.log2(math.e)

FFN_TM = 512
FFN_CHUNKS = ((0, 1536), (1536, 1280))
PRE_TM = 512
HALO = 32
GATE_COLS = 512
TQ = 256
CK = 256
CNT_ROWS = 32


def _rms(x, g):
    return x * lax.rsqrt(jnp.mean(x * x, axis=-1, keepdims=True) + EPS) * g


def _layer_norm(x, g, b):
    mu = jnp.mean(x, axis=-1, keepdims=True)
    xc = x - mu
    var = jnp.mean(xc * xc, axis=-1, keepdims=True)
    return xc * lax.rsqrt(var + EPS) * g + b


def _dot(a, b):
    return jnp.dot(a, b, preferred_element_type=F32)


def _dot_nt(a, b):
    return lax.dot_general(a, b, (((1,), (1,)), ((), ())), preferred_element_type=F32)


def _const_spec(shape):
    nd = len(shape)
    return pl.BlockSpec(shape, lambda *_: (0,) * nd, pipeline_mode=pl.Buffered(1))


def _ffn_body(x, gpre_ref, gpost_ref, wg_ref, wu_ref, wd_ref):
    h = _rms(x, gpre_ref[...]).astype(BF16)
    acc = None
    for start, size in FFN_CHUNKS:
        g = _dot(h, wg_ref[:, start:start + size])
        u = _dot(h, wu_ref[:, start:start + size])
        a = (g * jax.nn.sigmoid(g) * u).astype(BF16)
        part = _dot(a, wd_ref[start:start + size, :])
        acc = part if acc is None else acc + part
    return x + 0.5 * _rms(acc, gpost_ref[...])


def _ffn_kernel(x_ref, gpre_ref, gpost_ref, wg_ref, wu_ref, wd_ref, o_ref):
    o_ref[...] = _ffn_body(x_ref[...], gpre_ref, gpost_ref, wg_ref, wu_ref, wd_ref)


def _post_ffn_kernel(x_ref, oa_ref, ga_ref, mbm_ref, wdo_ref, wout_ref, gmix_ref,
                     gpre_ref, gpost_ref, wg_ref, wu_ref, wd_ref, o_ref):
    y_a = _dot(oa_ref[...], wdo_ref[...])
    merged = ga_ref[...].astype(F32) * y_a + mbm_ref[...].astype(F32)
    x = x_ref[...] + _rms(_dot(merged.astype(BF16), wout_ref[...]), gmix_ref[...])
    o_ref[...] = _ffn_body(x, gpre_ref, gpost_ref, wg_ref, wu_ref, wd_ref)


def _ffn_weights(g_pre, g_post, w_gu, w_down):
    d = g_pre.shape[-1]
    return [g_pre.reshape(1, d), g_post.reshape(1, d), w_gu[:, :D_FF].astype(BF16),
            w_gu[:, D_FF:].astype(BF16), w_down.astype(BF16)]


def _ffn(x2d, g_pre, g_post, w_gu, w_down):
    n, d = x2d.shape
    assert n % FFN_TM == 0 and d == D_MODEL
    consts = _ffn_weights(g_pre, g_post, w_gu, w_down)
    row = pl.BlockSpec((FFN_TM, d), lambda i: (i, 0))
    return pl.pallas_call(
        _ffn_kernel,
        grid=(n // FFN_TM,),
        in_specs=[row] + [_const_spec(c.shape) for c in consts],
        out_specs=row,
        out_shape=jax.ShapeDtypeStruct((n, d), F32),
        compiler_params=pltpu.CompilerParams(
            dimension_semantics=("arbitrary",), vmem_limit_bytes=VMEM_LIMIT),
        name="ffn",
    )(x2d, *consts)


def _post_ffn(x2d, oa2d, ga2d, mbm2d, p, g_pre, g_post, w_gu, w_down):
    n, d = x2d.shape
    assert n % FFN_TM == 0 and d == D_MODEL
    consts = [p["w_dsa_o"], p["w_out"], p["mix_post_g"]] + _ffn_weights(g_pre, g_post, w_gu, w_down)

    def row(width):
        return pl.BlockSpec((FFN_TM, width), lambda i: (i, 0))

    return pl.pallas_call(
        _post_ffn_kernel,
        grid=(n // FFN_TM,),
        in_specs=[row(d), row(oa2d.shape[1]), row(d), row(d)] + [_const_spec(c.shape) for c in consts],
        out_specs=row(d),
        out_shape=jax.ShapeDtypeStruct((n, d), F32),
        compiler_params=pltpu.CompilerParams(
            dimension_semantics=("arbitrary",), vmem_limit_bytes=VMEM_LIMIT),
        name="post_ffn",
    )(x2d, oa2d, ga2d, mbm2d, *consts)


def _mix_pre_kernel(x_ref, mem_ref, gpre_ref, wcq_ref, wckv_ref, wsm_ref, wglu_ref, wqm_ref,
                    wgate_ref, qng_ref, kvng_ref, ilg_ref, ilb_ref, wdw_ref, bdw_ref, clg_ref,
                    clb_ref, wco_ref, mng_ref, wmkv_ref, wmo_ref,
                    cq_ref, ckv_ref, ckvt_ref, kidx_ref, widxt_ref, ga_ref, mbm_ref,
                    ubuf_ref, ushift_ref, kmem_ref, vmem_ref):
    j = pl.program_id(1)
    tm = x_ref.shape[1]
    hm = N_HEADS_M * D_HEAD_M

    @pl.when(j == 0)
    def _():
        ubuf_ref[0:HALO, :] = jnp.zeros((HALO, CONV_CH), F32)
        mk = _rms(mem_ref[0], mng_ref[...]).astype(BF16)
        kv = _dot(mk, wmkv_ref[...])
        kmem_ref[...] = kv[:, :hm].astype(BF16)
        vmem_ref[...] = kv[:, hm:].astype(BF16)

    h = _rms(x_ref[0], gpre_ref[...]).astype(BF16)

    uglu = _dot(h, wglu_ref[...])
    ubuf_ref[HALO:HALO + tm, :] = uglu[:, :CONV_CH] * jax.nn.sigmoid(uglu[:, CONV_CH:])
    span = tm + HALO - SUBLANES
    for r in range(1, SUBLANES):
        ushift_ref[r - 1] = ubuf_ref[r:r + span, :]

    first = HALO - (CONV_WIDTH - 1)
    y = bdw_ref[...]
    for t in range(CONV_WIDTH):
        shift, base = (first + t) % SUBLANES, (first + t) // SUBLANES * SUBLANES
        rows = ubuf_ref[base:base + tm, :] if shift == 0 else ushift_ref[shift - 1, base:base + tm, :]
        y = y + wdw_ref[t:t + 1, :] * rows
    ubuf_ref[0:HALO, :] = ubuf_ref[tm:tm + HALO, :]
    z = _layer_norm(y, clg_ref[...], clb_ref[...])
    y_b = _dot((z * jax.nn.sigmoid(z)).astype(BF16), wco_ref[...])

    cq_ref[0] = _rms(_dot(h, wcq_ref[...]), qng_ref[...]).astype(BF16)
    ckv = _rms(_dot(h, wckv_ref[...]), kvng_ref[...])
    ckv_ref[0] = ckv.astype(BF16)
    for k in range(tm // CK):
        ckvt_ref[0, k, 0:KV_LORA, :] = ckv[k * CK:(k + 1) * CK, :].T.astype(BF16)
        ckvt_ref[0, k, KV_LORA:KV_LORA + BF16_ROWS, :] = jnp.ones((BF16_ROWS, CK), BF16)
    small = _dot(h, wsm_ref[...])
    kidx_ref[0] = _layer_norm(small[:, :D_IDX], ilg_ref[...], ilb_ref[...]).astype(BF16)
    small_t = small.T
    widxt_ref[0] = small_t[D_IDX:D_IDX + SUBLANES, :] * (N_IDX ** -0.5)

    qm = _dot(h, wqm_ref[...]).astype(BF16)
    heads = []
    for hh in range(N_HEADS_M):
        sl = slice(hh * D_HEAD_M, (hh + 1) * D_HEAD_M)
        s = _dot_nt(qm[:, sl], kmem_ref[:, sl]) * (D_HEAD_M ** -0.5)
        p = jnp.exp(s - jnp.max(s, axis=-1, keepdims=True))
        o = _dot(p.astype(BF16), vmem_ref[:, sl])
        heads.append((o / jnp.sum(p, axis=-1, keepdims=True)).astype(BF16))
    y_m = _dot(jnp.concatenate(heads, axis=-1), wmo_ref[...])

    for c0 in range(0, D_MODEL, GATE_COLS):
        def gate(branch):
            return jax.nn.sigmoid(_dot(h, wgate_ref[:, branch * D_MODEL + c0:branch * D_MODEL + c0 + GATE_COLS]))
        ga_ref[0, :, c0:c0 + GATE_COLS] = gate(0).astype(BF16)
        mbm_ref[0, :, c0:c0 + GATE_COLS] = (gate(1) * y_b[:, c0:c0 + GATE_COLS]
                                            + gate(2) * y_m[:, c0:c0 + GATE_COLS]).astype(BF16)


def _mix_pre(x, mem, p):
    b, t, d = x.shape
    tm = min(PRE_TM, t)
    assert t % tm == 0 and tm % CK == 0 and d == D_MODEL and mem.shape == (b, N_MEM, d)
    hm = N_HEADS_M * D_HEAD_M

    def tile(width):
        return pl.BlockSpec((1, tm, width), lambda i, j: (i, j, 0))

    consts = [p["mix_pre_g"], p["w_cq"], p["w_ckv"], p["w_small"], p["w_glu"], p["w_qmem"],
              p["w_gate"], p["q_norm_g"], p["kv_norm_g"], p["idx_ln_g"], p["idx_ln_b"], p["w_dw"],
              p["b_dw"], p["conv_ln_g"], p["conv_ln_b"], p["w_conv_out"], p["mem_norm_g"],
              p["w_mem_kv"], p["w_mem_o"]]
    kvt_rows = KV_LORA + BF16_ROWS
    out_shape = (
        jax.ShapeDtypeStruct((b, t, Q_LORA), BF16),
        jax.ShapeDtypeStruct((b, t, KV_LORA), BF16),
        jax.ShapeDtypeStruct((b, t // CK, kvt_rows, CK), BF16),
        jax.ShapeDtypeStruct((b, t, D_IDX), BF16),
        jax.ShapeDtypeStruct((b, SUBLANES, t), F32),
        jax.ShapeDtypeStruct((b, t, d), BF16),
        jax.ShapeDtypeStruct((b, t, d), BF16),
    )
    out_specs = (
        tile(Q_LORA), tile(KV_LORA),
        pl.BlockSpec((1, tm // CK, kvt_rows, CK), lambda i, j: (i, j, 0, 0)),
        tile(D_IDX),
        pl.BlockSpec((1, SUBLANES, tm), lambda i, j: (i, 0, j)),
        tile(d), tile(d),
    )
    return pl.pallas_call(
        _mix_pre_kernel,
        grid=(b, t // tm),
        in_specs=[tile(d), pl.BlockSpec((1, N_MEM, d), lambda i, j: (i, 0, 0))]
        + [_const_spec(c.shape) for c in consts],
        out_specs=out_specs,
        out_shape=out_shape,
        scratch_shapes=[pltpu.VMEM((HALO + tm, CONV_CH), F32),
                        pltpu.VMEM((SUBLANES - 1, HALO + tm - SUBLANES, CONV_CH), F32),
                        pltpu.VMEM((N_MEM, hm), BF16), pltpu.VMEM((N_MEM, hm), BF16)],
        compiler_params=pltpu.CompilerParams(
            dimension_semantics=("arbitrary", "arbitrary"), vmem_limit_bytes=VMEM_LIMIT),
        name="mix_pre",
    )(x, mem, *consts)


def _mix_dsa_kernel(topk, cq_ref, widxt_ref, ckv_ref, ckvt_ref, kidx_ref,
                    wiqt_ref, wuqt_ref, wuk_ref, wuvt_ref, ltri_ref,
                    o_ref, sc_ref, acc_ref, qlat_ref, s_ref):
    qi = pl.program_id(1)
    nch = (qi * TQ + TQ + CK - 1) // CK

    cq = cq_ref[0]
    qpos = qi * TQ + lax.broadcasted_iota(I32, (CK, TQ), 1)
    krow = lax.broadcasted_iota(I32, (CK, TQ), 0)

    qidx_t = (_dot_nt(wiqt_ref[...], cq) * (D_IDX ** -0.5)).astype(BF16)
    widx_t = widxt_ref[0]

    def fold(x, op):
        return op(x.reshape(CK // CNT_ROWS, CNT_ROWS, TQ), axis=0)

    def score_body(c, carry):
        n_ge0, n_gt0, s_max, s_min = carry
        start = pl.multiple_of(c * CK, CK)
        kc = kidx_ref[0, pl.ds(start, CK), :]
        score = None
        for hh in range(N_IDX):
            lg = _dot(kc, qidx_t[hh * D_IDX:(hh + 1) * D_IDX, :])
            term = widx_t[hh:hh + 1, :] * jnp.maximum(lg, 0.0)
            score = term if score is None else score + term
        causal = krow + start <= qpos
        score = jnp.where(causal, score + 0.0, NEG_INF)
        sc_ref[pl.ds(start, CK), :] = score
        return (n_ge0 + fold(jnp.where(score >= 0.0, 1, 0), jnp.sum),
                n_gt0 + fold(jnp.where(score > 0.0, 1, 0), jnp.sum),
                jnp.maximum(s_max, fold(score, jnp.max)),
                jnp.minimum(s_min, fold(jnp.where(causal, score, jnp.inf), jnp.min)))

    zeros_i = jnp.zeros((CNT_ROWS, TQ), I32)
    stats = lax.fori_loop(0, nch, score_body,
                          (zeros_i, zeros_i, jnp.full((CNT_ROWS, TQ), -jnp.inf, F32),
                           jnp.full((CNT_ROWS, TQ), jnp.inf, F32)))
    n_ge0 = jnp.sum(stats[0], axis=0, keepdims=True)
    n_gt0 = jnp.sum(stats[1], axis=0, keepdims=True)
    s_max = jnp.max(stats[2], axis=0, keepdims=True)
    s_min = jnp.min(stats[3], axis=0, keepdims=True)
    n_causal = qpos[0:1, :] + 1

    def count_ge(thr):
        def body(c, cnt):
            start = pl.multiple_of(c * CK, CK)
            return cnt + fold(jnp.where(sc_ref[pl.ds(start, CK), :] >= thr, 1, 0), jnp.sum)
        return jnp.sum(lax.fori_loop(0, nch, body, zeros_i), axis=0, keepdims=True)

    few = n_causal < topk
    zero_tie = jnp.where(n_gt0 < topk, jnp.where(n_ge0 >= topk, 1, 0), 0) == 1
    positive = n_gt0 >= topk
    lo = jnp.where(few, NEG_INF, jnp.where(positive, 0.0, jnp.where(zero_tie, 0.0, s_min)))
    hi = jnp.where(positive, jnp.minimum(s_max * (1.0 + 1e-6) + 1e-30, F32_BIG), 0.0)
    c_lo = jnp.where(few, topk, jnp.where(positive, n_ge0, jnp.where(zero_tie, n_ge0, n_causal)))
    c_hi = jnp.where(positive, 0, jnp.where(zero_tie, n_gt0, n_ge0))
    done = jnp.where(few, 1, jnp.where(zero_tie, 1, jnp.where(c_lo == topk, 1, 0)))

    def halve(state):
        lo, hi, c_lo, c_hi, done = state
        mid = lo + (hi - lo) * 0.5
        stuck = jnp.where(mid <= lo, 1, jnp.where(mid >= hi, 1, 0))
        c = count_ge(mid)
        live = jnp.where(done == 0, 1 - stuck, 0)
        raise_lo = jnp.where(c >= topk, live, 0) == 1
        lower_hi = jnp.where(c >= topk, 0, live) == 1
        lo, c_lo = jnp.where(raise_lo, mid, lo), jnp.where(raise_lo, c, c_lo)
        hi, c_hi = jnp.where(lower_hi, mid, hi), jnp.where(lower_hi, c, c_hi)
        return lo, hi, c_lo, c_hi, jnp.where(c_lo == topk, 1, jnp.maximum(done, stuck))

    state = lax.fori_loop(0, SEARCH_UNCHECKED, lambda _, st: halve(st), (lo, hi, c_lo, c_hi, done))

    def checked_halve(carry):
        state = halve(carry[2])
        return carry[0] + 1, jnp.min(state[4]), state

    _, _, (kth, _, n_ge, c_hi, _) = lax.while_loop(
        lambda carry: jnp.logical_and(carry[0] < SEARCH_CAP, carry[1] == 0), checked_halve,
        (jnp.int32(0), jnp.min(state[4]), state))
    n_tie_f = jnp.where(n_ge == topk, topk, topk - c_hi).astype(F32)

    def mask_body(ranked, c, tie_seen):
        start = pl.multiple_of(c * CK, CK)
        sc = sc_ref[pl.ds(start, CK), :]
        if ranked:
            tie = sc == kth
            tie_rank = _dot(ltri_ref[...], jnp.where(tie, 1.0, 0.0).astype(BF16)) + tie_seen
            bias = jnp.where(sc > kth, 0.0,
                             jnp.where(tie, jnp.where(tie_rank <= n_tie_f, 0.0, NEG_INF), NEG_INF))
            tie_seen = tie_rank[CK - 1:CK, :]
        else:
            bias = jnp.where(sc >= kth, 0.0, NEG_INF)
        sc_ref[pl.ds(start, CK), :] = jnp.where(krow + start <= qpos, bias, NEG_INF)
        return tie_seen

    ranked = jnp.max(jnp.where(n_ge > topk, 1, 0)) > 0
    no_ties = jnp.zeros((1, TQ), F32)

    @pl.when(ranked)
    def _():
        lax.fori_loop(0, nch, functools.partial(mask_body, True), no_ties)

    @pl.when(jnp.logical_not(ranked))
    def _():
        lax.fori_loop(0, nch, functools.partial(mask_body, False), no_ties)

    q_t = _dot_nt(wuqt_ref[...], cq).astype(BF16)
    for hh in range(N_HEADS_A):
        qlat_ref[hh] = (_dot(wuk_ref[hh], q_t[hh * D_QK:(hh + 1) * D_QK, :])
                        * (D_QK ** -0.5 * LOG2E)).astype(BF16)

    acc_ref[...] = jnp.zeros_like(acc_ref)

    def scores(c, hh):
        start = pl.multiple_of(c * CK, CK)
        s = _dot(ckv_ref[0, pl.ds(start, CK), :], qlat_ref[hh]) + sc_ref[pl.ds(start, CK), :]
        s_ref[hh] = s
        return jnp.max(s, axis=0, keepdims=True)

    def accumulate(c, hh, s_max, m_run, l_run):
        m_new = jnp.maximum(m_run, s_max)
        p = jnp.exp2(s_ref[hh] - jnp.maximum(m_new, 0.1 * NEG_INF)).astype(BF16)
        alpha = jnp.exp2(m_run - m_new)
        pv = _dot(ckvt_ref[0, c], p)
        acc_ref[hh] = alpha * acc_ref[hh] + pv[0:KV_LORA, :]
        return m_new, alpha * l_run + pv[KV_LORA:KV_LORA + 1, :]

    def attn_body(c, carry):
        m_run, l_run, s_max = carry
        m_out, l_out, s_out = [], [], []
        for hh in range(N_HEADS_A):
            m_new, l_new = accumulate(c - 1, hh, s_max[hh], m_run[hh], l_run[hh])
            s_out.append(scores(c, hh))
            m_out.append(m_new)
            l_out.append(l_new)
        return tuple(m_out), tuple(l_out), tuple(s_out)

    heads = range(N_HEADS_A)
    init = (tuple(jnp.full((1, TQ), NEG_INF, F32) for _ in heads),
            tuple(jnp.zeros((1, TQ), F32) for _ in heads),
            tuple(scores(0, hh) for hh in heads))
    m_run, l_run, s_max = lax.fori_loop(1, nch, attn_body, init)
    l_fin = [accumulate(nch - 1, hh, s_max[hh], m_run[hh], l_run[hh])[1] for hh in heads]

    o_t = jnp.concatenate(
        [_dot(wuvt_ref[hh], (acc_ref[hh] * (1.0 / l_fin[hh])).astype(BF16)) for hh in range(N_HEADS_A)],
        axis=0)
    o_ref[0] = o_t.T.astype(BF16)


def _mix_dsa(pre, p, t):
    cq, ckv, ckvt, kidx, widxt = pre
    b = cq.shape[0]
    assert t % CK == 0 and t % TQ == 0
    topk = min(TOPK_MAX, t // 4)
    assert topk <= CK
    ltri = (lax.broadcasted_iota(I32, (CK, CK), 1) <= lax.broadcasted_iota(I32, (CK, CK), 0)).astype(BF16)
    consts = [p["w_idx_q_t"], p["w_uq_t"], p["w_uk_h"], p["w_uv_t"], ltri]

    def whole(shape):
        nd = len(shape)
        return pl.BlockSpec((1,) + tuple(shape[1:]), lambda i, j: (i,) + (0,) * (nd - 1))

    return pl.pallas_call(
        functools.partial(_mix_dsa_kernel, topk),
        grid=(b, t // TQ),
        in_specs=[pl.BlockSpec((1, TQ, Q_LORA), lambda i, j: (i, j, 0)),
                  pl.BlockSpec((1, SUBLANES, TQ), lambda i, j: (i, 0, j)),
                  whole(ckv.shape), whole(ckvt.shape), whole(kidx.shape)]
        + [_const_spec(c.shape) for c in consts],
        out_specs=pl.BlockSpec((1, TQ, N_HEADS_A * D_V), lambda i, j: (i, j, 0)),
        out_shape=jax.ShapeDtypeStruct((b, t, N_HEADS_A * D_V), BF16),
        scratch_shapes=[pltpu.VMEM((t, TQ), F32), pltpu.VMEM((N_HEADS_A, KV_LORA, TQ), F32),
                        pltpu.VMEM((N_HEADS_A, KV_LORA, TQ), BF16), pltpu.VMEM((N_HEADS_A, CK, TQ), F32)],
        compiler_params=pltpu.CompilerParams(
            dimension_semantics=("arbitrary", "arbitrary"), vmem_limit_bytes=VMEM_LIMIT),
        name="mix_dsa",
    )(cq, widxt, ckv, ckvt, kidx, *consts)


def _mixer_params(mix_pre_g, mix_post_g, w_in, q_norm_g, kv_norm_g, w_uq, w_uk, w_uv, w_idx_q,
                  idx_ln_g, idx_ln_b, w_dsa_o, w_dw, b_dw, conv_ln_g, conv_ln_b, w_conv_out,
                  mem_norm_g, w_mem_kv, w_mem_o, w_out):
    hm = N_HEADS_M * D_HEAD_M
    cuts = [0]
    for width in (Q_LORA, KV_LORA, D_IDX, N_IDX, 2 * CONV_CH, hm, 3 * D_MODEL):
        cuts.append(cuts[-1] + width)
    assert w_in.shape == (D_MODEL, cuts[-1])
    w_small = jnp.pad(w_in[:, cuts[2]:cuts[4]], ((0, 0), (0, LANES - D_IDX - N_IDX)))
    row = lambda v: v.reshape(1, -1).astype(F32)
    return {
        "mix_pre_g": row(mix_pre_g), "mix_post_g": row(mix_post_g),
        "w_cq": w_in[:, cuts[0]:cuts[1]].astype(BF16), "w_ckv": w_in[:, cuts[1]:cuts[2]].astype(BF16),
        "w_small": w_small.astype(BF16),
        "w_glu": w_in[:, cuts[4]:cuts[5]].astype(BF16), "w_qmem": w_in[:, cuts[5]:cuts[6]].astype(BF16),
        "w_gate": w_in[:, cuts[6]:cuts[7]].astype(BF16),
        "q_norm_g": row(q_norm_g), "kv_norm_g": row(kv_norm_g),
        "idx_ln_g": row(idx_ln_g), "idx_ln_b": row(idx_ln_b),
        "w_dw": w_dw.astype(F32), "b_dw": row(b_dw),
        "conv_ln_g": row(conv_ln_g), "conv_ln_b": row(conv_ln_b),
        "w_conv_out": w_conv_out.astype(BF16), "mem_norm_g": row(mem_norm_g),
        "w_mem_kv": w_mem_kv.astype(BF16), "w_mem_o": w_mem_o.astype(BF16),
        "w_idx_q_t": w_idx_q.T.astype(BF16), "w_uq_t": w_uq.T.astype(BF16),
        "w_uk_h": jnp.transpose(w_uk, (1, 0, 2)).astype(BF16),
        "w_uv_t": jnp.transpose(w_uv, (1, 2, 0)).astype(BF16),
        "w_dsa_o": w_dsa_o.astype(BF16), "w_out": w_out.astype(BF16),
    }


def kernel(x, mem, ffn1_pre_g, ffn1_post_g, ffn1_w_gu, ffn1_w_down, mix_pre_g, mix_post_g, w_in, q_norm_g, kv_norm_g, w_uq, w_uk, w_uv, w_idx_q, idx_ln_g, idx_ln_b, w_dsa_o, w_dw, b_dw, conv_ln_g, conv_ln_b, w_conv_out, mem_norm_g, w_mem_kv, w_mem_o, w_out, ffn2_pre_g, ffn2_post_g, ffn2_w_gu, ffn2_w_down):
    b, t, d = x.shape
    n = b * t
    for l in range(ffn1_pre_g.shape[0]):
        x = _ffn(x.reshape(n, d), ffn1_pre_g[l], ffn1_post_g[l], ffn1_w_gu[l], ffn1_w_down[l])
        p = _mixer_params(mix_pre_g[l], mix_post_g[l], w_in[l], q_norm_g[l], kv_norm_g[l], w_uq[l],
                          w_uk[l], w_uv[l], w_idx_q[l], idx_ln_g[l], idx_ln_b[l], w_dsa_o[l], w_dw[l],
                          b_dw[l], conv_ln_g[l], conv_ln_b[l], w_conv_out[l], mem_norm_g[l],
                          w_mem_kv[l], w_mem_o[l], w_out[l])
        cq, ckv, ckvt, kidx, widxt, ga, mbm = _mix_pre(x.reshape(b, t, d), mem, p)
        oa = _mix_dsa((cq, ckv, ckvt, kidx, widxt), p, t)
        x = _post_ffn(x, oa.reshape(n, -1), ga.reshape(n, d), mbm.reshape(n, d), p,
                      ffn2_pre_g[l], ffn2_post_g[l], ffn2_w_gu[l], ffn2_w_down[l])
        x = x.reshape(b, t, d)
    return x
```

```python
import functools
import math

import jax
import jax.numpy as jnp
from jax import lax
from jax.experimental import pallas as pl
from jax.experimental.pallas import tpu as pltpu

F32 = jnp.float32
BF16 = jnp.bfloat16
I32 = jnp.int32

EPS = 1e-6
NEG_INF = -1e30
D_MODEL = 1024
D_FF = 2816
N_HEADS_A = 8
D_QK = 64
D_V = 64
Q_LORA = 256
KV_LORA = 256
N_IDX = 4
D_IDX = 64
TOPK_MAX = 256
CONV_CH = 512
CONV_WIDTH = 31
N_HEADS_M = 4
D_HEAD_M = 128
N_MEM = 256

LANES = 128
SUBLANES = 8
BF16_ROWS = 16
VMEM_LIMIT = 56 * 1024 * 1024
F32_BIG = 3.0e38
SEARCH_CAP = 288
SEARCH_UNCHECKED = 19
LOG2E = math.log2(math.e)

FFN_TM = 512
FFN_CHUNKS = ((0, 1536), (1536, 1280))
PRE_TM = 512
HALO = 32
GATE_COLS = 512
CONV_ROWS = 32
TQ = 256
CK = 256
CNT_ROWS = 32


def _rms(x, g):
    return x * lax.rsqrt(jnp.mean(x * x, axis=-1, keepdims=True) + EPS) * g


def _layer_norm(x, g, b):
    mu = jnp.mean(x, axis=-1, keepdims=True)
    xc = x - mu
    var = jnp.mean(xc * xc, axis=-1, keepdims=True)
    return xc * lax.rsqrt(var + EPS) * g + b


def _dot(a, b):
    return jnp.dot(a, b, preferred_element_type=F32)


def _dot_nt(a, b):
    return lax.dot_general(a, b, (((1,), (1,)), ((), ())), preferred_element_type=F32)


def _const_spec(shape):
    nd = len(shape)
    return pl.BlockSpec(shape, lambda *_: (0,) * nd, pipeline_mode=pl.Buffered(1))


def _ffn_body(x, gpre_ref, gpost_ref, wg_ref, wu_ref, wd_ref):
    h = _rms(x, gpre_ref[...]).astype(BF16)
    acc = None
    for start, size in FFN_CHUNKS:
        g = _dot(h, wg_ref[:, start:start + size])
        u = _dot(h, wu_ref[:, start:start + size])
        a = (g * jax.nn.sigmoid(g) * u).astype(BF16)
        part = _dot(a, wd_ref[start:start + size, :])
        acc = part if acc is None else acc + part
    return x + 0.5 * _rms(acc, gpost_ref[...])


def _ffn_kernel(x_ref, gpre_ref, gpost_ref, wg_ref, wu_ref, wd_ref, o_ref):
    o_ref[...] = _ffn_body(x_ref[...], gpre_ref, gpost_ref, wg_ref, wu_ref, wd_ref)


def _post_ffn_kernel(x_ref, oa_ref, ga_ref, mbm_ref, wdo_ref, wout_ref, gmix_ref,
                     gpre_ref, gpost_ref, wg_ref, wu_ref, wd_ref, o_ref):
    y_a = _dot(oa_ref[...], wdo_ref[...])
    merged = ga_ref[...].astype(F32) * y_a + mbm_ref[...].astype(F32)
    x = x_ref[...] + _rms(_dot(merged.astype(BF16), wout_ref[...]), gmix_ref[...])
    o_ref[...] = _ffn_body(x, gpre_ref, gpost_ref, wg_ref, wu_ref, wd_ref)


def _ffn_weights(g_pre, g_post, w_gu, w_down):
    d = g_pre.shape[-1]
    return [g_pre.reshape(1, d), g_post.reshape(1, d), w_gu[:, :D_FF].astype(BF16),
            w_gu[:, D_FF:].astype(BF16), w_down.astype(BF16)]


def _ffn(x2d, g_pre, g_post, w_gu, w_down):
    n, d = x2d.shape
    assert n % FFN_TM == 0 and d == D_MODEL
    consts = _ffn_weights(g_pre, g_post, w_gu, w_down)
    row = pl.BlockSpec((FFN_TM, d), lambda i: (i, 0))
    return pl.pallas_call(
        _ffn_kernel,
        grid=(n // FFN_TM,),
        in_specs=[row] + [_const_spec(c.shape) for c in consts],
        out_specs=row,
        out_shape=jax.ShapeDtypeStruct((n, d), F32),
        compiler_params=pltpu.CompilerParams(
            dimension_semantics=("arbitrary",), vmem_limit_bytes=VMEM_LIMIT),
        name="ffn",
    )(x2d, *consts)


def _post_ffn(x2d, oa2d, ga2d, mbm2d, p, g_pre, g_post, w_gu, w_down):
    n, d = x2d.shape
    assert n % FFN_TM == 0 and d == D_MODEL
    consts = [p["w_dsa_o"], p["w_out"], p["mix_post_g"]] + _ffn_weights(g_pre, g_post, w_gu, w_down)

    def row(width):
        return pl.BlockSpec((FFN_TM, width), lambda i: (i, 0))

    return pl.pallas_call(
        _post_ffn_kernel,
        grid=(n // FFN_TM,),
        in_specs=[row(d), row(oa2d.shape[1]), row(d), row(d)] + [_const_spec(c.shape) for c in consts],
        out_specs=row(d),
        out_shape=jax.ShapeDtypeStruct((n, d), F32),
        compiler_params=pltpu.CompilerParams(
            dimension_semantics=("arbitrary",), vmem_limit_bytes=VMEM_LIMIT),
        name="post_ffn",
    )(x2d, oa2d, ga2d, mbm2d, *consts)


def _mix_pre_kernel(x_ref, mem_ref, gpre_ref, wcq_ref, wckv_ref, wsm_ref, wglu_ref, wqm_ref,
                    wgate_ref, qng_ref, kvng_ref, ilg_ref, ilb_ref, wdw_ref, bdw_ref, clg_ref,
                    clb_ref, wco_ref, mng_ref, wmkv_ref, wmo_ref,
                    cq_ref, ckv_ref, ckvt_ref, kidx_ref, widxt_ref, ga_ref, mbm_ref,
                    ubuf_ref, ushift_ref, kmem_ref, vmem_ref, gbm_ref, yconv_ref, wtap_ref):
    j = pl.program_id(1)
    tm = x_ref.shape[1]
    hm = N_HEADS_M * D_HEAD_M

    @pl.when(j == 0)
    def _():
        ubuf_ref[0:HALO, :] = jnp.zeros((HALO, CONV_CH), F32)
        for t in range(CONV_WIDTH):
            wtap_ref[t] = jnp.broadcast_to(wdw_ref[t:t + 1, :], (SUBLANES, CONV_CH))
        mk = _rms(mem_ref[0], mng_ref[...]).astype(BF16)
        kv = _dot(mk, wmkv_ref[...])
        kmem_ref[...] = kv[:, :hm].astype(BF16)
        vmem_ref[...] = kv[:, hm:].astype(BF16)

    h = _rms(x_ref[0], gpre_ref[...]).astype(BF16)

    uglu = _dot(h, wglu_ref[...])
    ubuf_ref[HALO:HALO + tm, :] = uglu[:, :CONV_CH] * jax.nn.sigmoid(uglu[:, CONV_CH:])
    span = tm + HALO - SUBLANES
    for r in range(1, SUBLANES):
        ushift_ref[r - 1] = ubuf_ref[r:r + span, :]

    ties = []
    for c0 in range(0, 3 * D_MODEL, GATE_COLS):
        g = jax.nn.sigmoid(_dot(h, wgate_ref[:, c0:c0 + GATE_COLS]))
        if c0 < D_MODEL:
            ga_ref[0, :, c0:c0 + GATE_COLS] = g.astype(BF16)
        else:
            gbm_ref[:, c0 - D_MODEL:c0 - D_MODEL + GATE_COLS] = g
        ties.append(g[0:SUBLANES, 0:LANES] * 0.0)

    first = HALO - (CONV_WIDTH - 1)
    n_blocks = tm // CONV_ROWS
    for blk in range(n_blocks):
        r0 = blk * CONV_ROWS
        y = None
        for t in range(CONV_WIDTH):
            shift, base = (first + t) % SUBLANES, (first + t) // SUBLANES * SUBLANES + r0
            rows = (ubuf_ref[base:base + CONV_ROWS, :] if shift == 0
                    else ushift_ref[shift - 1, base:base + CONV_ROWS, :])
            term = wtap_ref[t][None] * rows.reshape(CONV_ROWS // SUBLANES, SUBLANES, CONV_CH)
            y = term if y is None else y + term
        tie = jnp.tile(ties[blk * len(ties) // n_blocks], (CONV_ROWS // SUBLANES, CONV_CH // LANES))
        yconv_ref[r0:r0 + CONV_ROWS, :] = y.reshape(CONV_ROWS, CONV_CH) + bdw_ref[...] + tie
    ubuf_ref[0:HALO, :] = ubuf_ref[tm:tm + HALO, :]
    z = _layer_norm(yconv_ref[...], clg_ref[...], clb_ref[...])
    y_b = _dot((z * jax.nn.sigmoid(z)).astype(BF16), wco_ref[...])

    cq_ref[0] = _rms(_dot(h, wcq_ref[...]), qng_ref[...]).astype(BF16)
    ckv = _rms(_dot(h, wckv_ref[...]), kvng_ref[...])
    ckv_ref[0] = ckv.astype(BF16)
    for k in range(tm // CK):
        ckvt_ref[0, k, 0:KV_LORA, :] = ckv[k * CK:(k + 1) * CK, :].T.astype(BF16)
        ckvt_ref[0, k, KV_LORA:KV_LORA + BF16_ROWS, :] = jnp.ones((BF16_ROWS, CK), BF16)
    small = _dot(h, wsm_ref[...])
    kidx_ref[0] = _layer_norm(small[:, :D_IDX], ilg_ref[...], ilb_ref[...]).astype(BF16)
    small_t = small.T
    widxt_ref[0] = small_t[D_IDX:D_IDX + SUBLANES, :] * (N_IDX ** -0.5)

    qm = _dot(h, wqm_ref[...]).astype(BF16)
    heads = []
    for hh in range(N_HEADS_M):
        sl = slice(hh * D_HEAD_M, (hh + 1) * D_HEAD_M)
        s = _dot_nt(qm[:, sl], kmem_ref[:, sl]) * (D_HEAD_M ** -0.5)
        p = jnp.exp(s - jnp.max(s, axis=-1, keepdims=True))
        o = _dot(p.astype(BF16), vmem_ref[:, sl])
        heads.append((o / jnp.sum(p, axis=-1, keepdims=True)).astype(BF16))
    y_m = _dot(jnp.concatenate(heads, axis=-1), wmo_ref[...])

    mbm_ref[0] = (gbm_ref[:, :D_MODEL] * y_b + gbm_ref[:, D_MODEL:] * y_m).astype(BF16)


def _mix_pre(x, mem, p):
    b, t, d = x.shape
    tm = min(PRE_TM, t)
    assert t % tm == 0 and tm % CK == 0 and d == D_MODEL and mem.shape == (b, N_MEM, d)
    hm = N_HEADS_M * D_HEAD_M

    def tile(width):
        return pl.BlockSpec((1, tm, width), lambda i, j: (i, j, 0))

    consts = [p["mix_pre_g"], p["w_cq"], p["w_ckv"], p["w_small"], p["w_glu"], p["w_qmem"],
              p["w_gate"], p["q_norm_g"], p["kv_norm_g"], p["idx_ln_g"], p["idx_ln_b"], p["w_dw"],
              p["b_dw"], p["conv_ln_g"], p["conv_ln_b"], p["w_conv_out"], p["mem_norm_g"],
              p["w_mem_kv"], p["w_mem_o"]]
    kvt_rows = KV_LORA + BF16_ROWS
    out_shape = (
        jax.ShapeDtypeStruct((b, t, Q_LORA), BF16),
        jax.ShapeDtypeStruct((b, t, KV_LORA), BF16),
        jax.ShapeDtypeStruct((b, t // CK, kvt_rows, CK), BF16),
        jax.ShapeDtypeStruct((b, t, D_IDX), BF16),
        jax.ShapeDtypeStruct((b, SUBLANES, t), F32),
        jax.ShapeDtypeStruct((b, t, d), BF16),
        jax.ShapeDtypeStruct((b, t, d), BF16),
    )
    out_specs = (
        tile(Q_LORA), tile(KV_LORA),
        pl.BlockSpec((1, tm // CK, kvt_rows, CK), lambda i, j: (i, j, 0, 0)),
        tile(D_IDX),
        pl.BlockSpec((1, SUBLANES, tm), lambda i, j: (i, 0, j)),
        tile(d), tile(d),
    )
    return pl.pallas_call(
        _mix_pre_kernel,
        grid=(b, t // tm),
        in_specs=[tile(d), pl.BlockSpec((1, N_MEM, d), lambda i, j: (i, 0, 0))]
        + [_const_spec(c.shape) for c in consts],
        out_specs=out_specs,
        out_shape=out_shape,
        scratch_shapes=[pltpu.VMEM((HALO + tm, CONV_CH), F32),
                        pltpu.VMEM((SUBLANES - 1, HALO + tm - SUBLANES, CONV_CH), F32),
                        pltpu.VMEM((N_MEM, hm), BF16), pltpu.VMEM((N_MEM, hm), BF16),
                        pltpu.VMEM((tm, 2 * d), F32), pltpu.VMEM((tm, CONV_CH), F32),
                        pltpu.VMEM((CONV_WIDTH, SUBLANES, CONV_CH), F32)],
        compiler_params=pltpu.CompilerParams(
            dimension_semantics=("arbitrary", "arbitrary"), vmem_limit_bytes=VMEM_LIMIT),
        name="mix_pre",
    )(x, mem, *consts)


def _mix_dsa_kernel(topk, cq_ref, widxt_ref, ckv_ref, ckvt_ref, kidx_ref,
                    wiqt_ref, wuqt_ref, wuk_ref, wuvt_ref, ltri_ref,
                    o_ref, sc_ref, acc_ref, qlat_ref, s_ref):
    qi = pl.program_id(1)
    nch = (qi * TQ + TQ + CK - 1) // CK

    cq = cq_ref[0]
    qpos = qi * TQ + lax.broadcasted_iota(I32, (CK, TQ), 1)
    krow = lax.broadcasted_iota(I32, (CK, TQ), 0)

    qidx_t = (_dot_nt(wiqt_ref[...], cq) * (D_IDX ** -0.5)).astype(BF16)
    widx_t = widxt_ref[0]

    def fold(x, op):
        return op(x.reshape(CK // CNT_ROWS, CNT_ROWS, TQ), axis=0)

    def score_body(c, carry):
        n_ge0, n_gt0, s_max, s_min = carry
        start = pl.multiple_of(c * CK, CK)
        kc = kidx_ref[0, pl.ds(start, CK), :]
        score = None
        for hh in range(N_IDX):
            lg = _dot(kc, qidx_t[hh * D_IDX:(hh + 1) * D_IDX, :])
            term = widx_t[hh:hh + 1, :] * jnp.maximum(lg, 0.0)
            score = term if score is None else score + term
        causal = krow + start <= qpos
        score = jnp.where(causal, score + 0.0, NEG_INF)
        sc_ref[pl.ds(start, CK), :] = score
        return (n_ge0 + fold(jnp.where(score >= 0.0, 1, 0), jnp.sum),
                n_gt0 + fold(jnp.where(score > 0.0, 1, 0), jnp.sum),
                jnp.maximum(s_max, fold(score, jnp.max)),
                jnp.minimum(s_min, fold(jnp.where(causal, score, jnp.inf), jnp.min)))

    zeros_i = jnp.zeros((CNT_ROWS, TQ), I32)
    stats = lax.fori_loop(0, nch, score_body,
                          (zeros_i, zeros_i, jnp.full((CNT_ROWS, TQ), -jnp.inf, F32),
                           jnp.full((CNT_ROWS, TQ), jnp.inf, F32)))
    n_ge0 = jnp.sum(stats[0], axis=0, keepdims=True)
    n_gt0 = jnp.sum(stats[1], axis=0, keepdims=True)
    s_max = jnp.max(stats[2], axis=0, keepdims=True)
    s_min = jnp.min(stats[3], axis=0, keepdims=True)
    n_causal = qpos[0:1, :] + 1

    def count_ge(thr):
        def body(c, cnt):
            start = pl.multiple_of(c * CK, CK)
            return cnt + fold(jnp.where(sc_ref[pl.ds(start, CK), :] >= thr, 1, 0), jnp.sum)
        return jnp.sum(lax.fori_loop(0, nch, body, zeros_i), axis=0, keepdims=True)

    few = n_causal < topk
    zero_tie = jnp.where(n_gt0 < topk, jnp.where(n_ge0 >= topk, 1, 0), 0) == 1
    positive = n_gt0 >= topk
    lo = jnp.where(few, NEG_INF, jnp.where(positive, 0.0, jnp.where(zero_tie, 0.0, s_min)))
    hi = jnp.where(positive, jnp.minimum(s_max * (1.0 + 1e-6) + 1e-30, F32_BIG), 0.0)
    c_lo = jnp.where(few, topk, jnp.where(positive, n_ge0, jnp.where(zero_tie, n_ge0, n_causal)))
    c_hi = jnp.where(positive, 0, jnp.where(zero_tie, n_gt0, n_ge0))
    done = jnp.where(few, 1, jnp.where(zero_tie, 1, jnp.where(c_lo == topk, 1, 0)))

    def halve(state):
        lo, hi, c_lo, c_hi, done = state
        mid = lo + (hi - lo) * 0.5
        stuck = jnp.where(mid <= lo, 1, jnp.where(mid >= hi, 1, 0))
        c = count_ge(mid)
        live = jnp.where(done == 0, 1 - stuck, 0)
        raise_lo = jnp.where(c >= topk, live, 0) == 1
        lower_hi = jnp.where(c >= topk, 0, live) == 1
        lo, c_lo = jnp.where(raise_lo, mid, lo), jnp.where(raise_lo, c, c_lo)
        hi, c_hi = jnp.where(lower_hi, mid, hi), jnp.where(lower_hi, c, c_hi)
        return lo, hi, c_lo, c_hi, jnp.where(c_lo == topk, 1, jnp.maximum(done, stuck))

    state = lax.fori_loop(0, SEARCH_UNCHECKED, lambda _, st: halve(st), (lo, hi, c_lo, c_hi, done))

    def checked_halve(carry):
        state = halve(carry[2])
        return carry[0] + 1, jnp.min(state[4]), state

    _, _, (kth, _, n_ge, c_hi, _) = lax.while_loop(
        lambda carry: jnp.logical_and(carry[0] < SEARCH_CAP, carry[1] == 0), checked_halve,
        (jnp.int32(0), jnp.min(state[4]), state))
    n_tie_f = jnp.where(n_ge == topk, topk, topk - c_hi).astype(F32)

    def mask_body(ranked, c, tie_seen):
        start = pl.multiple_of(c * CK, CK)
        sc = sc_ref[pl.ds(start, CK), :]
        if ranked:
            tie = sc == kth
            tie_rank = _dot(ltri_ref[...], jnp.where(tie, 1.0, 0.0).astype(BF16)) + tie_seen
            bias = jnp.where(sc > kth, 0.0,
                             jnp.where(tie, jnp.where(tie_rank <= n_tie_f, 0.0, NEG_INF), NEG_INF))
            tie_seen = tie_rank[CK - 1:CK, :]
        else:
            bias = jnp.where(sc >= kth, 0.0, NEG_INF)
        sc_ref[pl.ds(start, CK), :] = jnp.where(krow + start <= qpos, bias, NEG_INF)
        return tie_seen

    ranked = jnp.max(jnp.where(n_ge > topk, 1, 0)) > 0
    no_ties = jnp.zeros((1, TQ), F32)

    @pl.when(ranked)
    def _():
        lax.fori_loop(0, nch, functools.partial(mask_body, True), no_ties)

    @pl.when(jnp.logical_not(ranked))
    def _():
        lax.fori_loop(0, nch, functools.partial(mask_body, False), no_ties)

    q_t = _dot_nt(wuqt_ref[...], cq).astype(BF16)
    for hh in range(N_HEADS_A):
        qlat_ref[hh] = (_dot(wuk_ref[hh], q_t[hh * D_QK:(hh + 1) * D_QK, :])
                        * (D_QK ** -0.5 * LOG2E)).astype(BF16)

    acc_ref[...] = jnp.zeros_like(acc_ref)

    def scores(c, hh):
        start = pl.multiple_of(c * CK, CK)
        s = _dot(ckv_ref[0, pl.ds(start, CK), :], qlat_ref[hh]) + sc_ref[pl.ds(start, CK), :]
        s_ref[hh] = s
        return jnp.max(s, axis=0, keepdims=True)

    def accumulate(c, hh, s_max, m_run, l_run):
        m_new = jnp.maximum(m_run, s_max)
        p = jnp.exp2(s_ref[hh] - jnp.maximum(m_new, 0.1 * NEG_INF)).astype(BF16)
        alpha = jnp.exp2(m_run - m_new)
        pv = _dot(ckvt_ref[0, c], p)
        acc_ref[hh] = alpha * acc_ref[hh] + pv[0:KV_LORA, :]
        return m_new, alpha * l_run + pv[KV_LORA:KV_LORA + 1, :]

    def attn_body(c, carry):
        m_run, l_run, s_max = carry
        m_out, l_out, s_out = [], [], []
        for hh in range(N_HEADS_A):
            m_new, l_new = accumulate(c - 1, hh, s_max[hh], m_run[hh], l_run[hh])
            s_out.append(scores(c, hh))
            m_out.append(m_new)
            l_out.append(l_new)
        return tuple(m_out), tuple(l_out), tuple(s_out)

    heads = range(N_HEADS_A)
    init = (tuple(jnp.full((1, TQ), NEG_INF, F32) for _ in heads),
            tuple(jnp.zeros((1, TQ), F32) for _ in heads),
            tuple(scores(0, hh) for hh in heads))
    m_run, l_run, s_max = lax.fori_loop(1, nch, attn_body, init)
    l_fin = [accumulate(nch - 1, hh, s_max[hh], m_run[hh], l_run[hh])[1] for hh in heads]

    o_t = jnp.concatenate(
        [_dot(wuvt_ref[hh], (acc_ref[hh] * (1.0 / l_fin[hh])).astype(BF16)) for hh in range(N_HEADS_A)],
        axis=0)
    o_ref[0] = o_t.T.astype(BF16)


def _mix_dsa(pre, p, t):
    cq, ckv, ckvt, kidx, widxt = pre
    b = cq.shape[0]
    assert t % CK == 0 and t % TQ == 0
    topk = min(TOPK_MAX, t // 4)
    assert topk <= CK
    ltri = (lax.broadcasted_iota(I32, (CK, CK), 1) <= lax.broadcasted_iota(I32, (CK, CK), 0)).astype(BF16)
    consts = [p["w_idx_q_t"], p["w_uq_t"], p["w_uk_h"], p["w_uv_t"], ltri]

    def whole(shape):
        nd = len(shape)
        return pl.BlockSpec((1,) + tuple(shape[1:]), lambda i, j: (i,) + (0,) * (nd - 1))

    return pl.pallas_call(
        functools.partial(_mix_dsa_kernel, topk),
        grid=(b, t // TQ),
        in_specs=[pl.BlockSpec((1, TQ, Q_LORA), lambda i, j: (i, j, 0)),
                  pl.BlockSpec((1, SUBLANES, TQ), lambda i, j: (i, 0, j)),
                  whole(ckv.shape), whole(ckvt.shape), whole(kidx.shape)]
        + [_const_spec(c.shape) for c in consts],
        out_specs=pl.BlockSpec((1, TQ, N_HEADS_A * D_V), lambda i, j: (i, j, 0)),
        out_shape=jax.ShapeDtypeStruct((b, t, N_HEADS_A * D_V), BF16),
        scratch_shapes=[pltpu.VMEM((t, TQ), F32), pltpu.VMEM((N_HEADS_A, KV_LORA, TQ), F32),
                        pltpu.VMEM((N_HEADS_A, KV_LORA, TQ), BF16), pltpu.VMEM((N_HEADS_A, CK, TQ), F32)],
        compiler_params=pltpu.CompilerParams(
            dimension_semantics=("arbitrary", "arbitrary"), vmem_limit_bytes=VMEM_LIMIT),
        name="mix_dsa",
    )(cq, widxt, ckv, ckvt, kidx, *consts)


def _mixer_params(mix_pre_g, mix_post_g, w_in, q_norm_g, kv_norm_g, w_uq, w_uk, w_uv, w_idx_q,
                  idx_ln_g, idx_ln_b, w_dsa_o, w_dw, b_dw, conv_ln_g, conv_ln_b, w_conv_out,
                  mem_norm_g, w_mem_kv, w_mem_o, w_out):
    hm = N_HEADS_M * D_HEAD_M
    cuts = [0]
    for width in (Q_LORA, KV_LORA, D_IDX, N_IDX, 2 * CONV_CH, hm, 3 * D_MODEL):
        cuts.append(cuts[-1] + width)
    assert w_in.shape == (D_MODEL, cuts[-1])
    w_small = jnp.pad(w_in[:, cuts[2]:cuts[4]], ((0, 0), (0, LANES - D_IDX - N_IDX)))
    row = lambda v: v.reshape(1, -1).astype(F32)
    return {
        "mix_pre_g": row(mix_pre_g), "mix_post_g": row(mix_post_g),
        "w_cq": w_in[:, cuts[0]:cuts[1]].astype(BF16), "w_ckv": w_in[:, cuts[1]:cuts[2]].astype(BF16),
        "w_small": w_small.astype(BF16),
        "w_glu": w_in[:, cuts[4]:cuts[5]].astype(BF16), "w_qmem": w_in[:, cuts[5]:cuts[6]].astype(BF16),
        "w_gate": w_in[:, cuts[6]:cuts[7]].astype(BF16),
        "q_norm_g": row(q_norm_g), "kv_norm_g": row(kv_norm_g),
        "idx_ln_g": row(idx_ln_g), "idx_ln_b": row(idx_ln_b),
        "w_dw": w_dw.astype(F32), "b_dw": row(b_dw),
        "conv_ln_g": row(conv_ln_g), "conv_ln_b": row(conv_ln_b),
        "w_conv_out": w_conv_out.astype(BF16), "mem_norm_g": row(mem_norm_g),
        "w_mem_kv": w_mem_kv.astype(BF16), "w_mem_o": w_mem_o.astype(BF16),
        "w_idx_q_t": w_idx_q.T.astype(BF16), "w_uq_t": w_uq.T.astype(BF16),
        "w_uk_h": jnp.transpose(w_uk, (1, 0, 2)).astype(BF16),
        "w_uv_t": jnp.transpose(w_uv, (1, 2, 0)).astype(BF16),
        "w_dsa_o": w_dsa_o.astype(BF16), "w_out": w_out.astype(BF16),
    }


def kernel(x, mem, ffn1_pre_g, ffn1_post_g, ffn1_w_gu, ffn1_w_down, mix_pre_g, mix_post_g, w_in, q_norm_g, kv_norm_g, w_uq, w_uk, w_uv, w_idx_q, idx_ln_g, idx_ln_b, w_dsa_o, w_dw, b_dw, conv_ln_g, conv_ln_b, w_conv_out, mem_norm_g, w_mem_kv, w_mem_o, w_out, ffn2_pre_g, ffn2_post_g, ffn2_w_gu, ffn2_w_down):
    b, t, d = x.shape
    n = b * t
    for l in range(ffn1_pre_g.shape[0]):
        x = _ffn(x.reshape(n, d), ffn1_pre_g[l], ffn1_post_g[l], ffn1_w_gu[l], ffn1_w_down[l])
        p = _mixer_params(mix_pre_g[l], mix_post_g[l], w_in[l], q_norm_g[l], kv_norm_g[l], w_uq[l],
                          w_uk[l], w_uv[l], w_idx_q[l], idx_ln_g[l], idx_ln_b[l], w_dsa_o[l], w_dw[l],
                          b_dw[l], conv_ln_g[l], conv_ln_b[l], w_conv_out[l], mem_norm_g[l],
                          w_mem_kv[l], w_mem_o[l], w_out[l])
        cq, ckv, ckvt, kidx, widxt, ga, mbm = _mix_pre(x.reshape(b, t, d), mem, p)
        oa = _mix_dsa((cq, ckv, ckvt, kidx, widxt), p, t)
        x = _post_ffn(x, oa.reshape(n, -1), ga.reshape(n, d), mbm.reshape(n, d), p,
                      ffn2_pre_g[l], ffn2_post_g[l], ffn2_w_gu[l], ffn2_w_down[l])
        x = x.reshape(b, t, d)
    return x
```

```python
import functools
import math

import jax
import jax.numpy as jnp
from jax import lax
from jax.experimental import pallas as pl
from jax.experimental.pallas import tpu as pltpu

F32 = jnp.float32
BF16 = jnp.bfloat16
I32 = jnp.int32

EPS = 1e-6
NEG_INF = -1e30
D_MODEL = 1024
D_FF = 2816
N_HEADS_A = 8
D_QK = 64
D_V = 64
Q_LORA = 256
KV_LORA = 256
N_IDX = 4
D_IDX = 64
TOPK_MAX = 256
CONV_CH = 512
CONV_WIDTH = 31
N_HEADS_M = 4
D_HEAD_M = 128
N_MEM = 256

LANES = 128
SUBLANES = 8
BF16_ROWS = 16
VMEM_LIMIT = 56 * 1024 * 1024
F32_BIG = 3.0e38
SEARCH_CAP = 288
SEARCH_UNCHECKED = 19
LOG2E = math.log2(math.e)

FFN_TM = 512
FFN_CHUNKS = ((0, 1536), (1536, 1280))
PRE_TM = 512
HALO = 32
GATE_COLS = 256
CONV_ROWS = 32
TQ = 256
CK = 256
CNT_ROWS = 32


def _rms(x, g):
    return x * lax.rsqrt(jnp.mean(x * x, axis=-1, keepdims=True) + EPS) * g


def _layer_norm(x, g, b):
    mu = jnp.mean(x, axis=-1, keepdims=True)
    xc = x - mu
    var = jnp.mean(xc * xc, axis=-1, keepdims=True)
    return xc * lax.rsqrt(var + EPS) * g + b


def _dot(a, b):
    return jnp.dot(a, b, preferred_element_type=F32)


def _dot_nt(a, b):
    return lax.dot_general(a, b, (((1,), (1,)), ((), ())), preferred_element_type=F32)


def _const_spec(shape):
    nd = len(shape)
    return pl.BlockSpec(shape, lambda *_: (0,) * nd, pipeline_mode=pl.Buffered(1))


def _ffn_body(x, gpre_ref, gpost_ref, wg_ref, wu_ref, wd_ref):
    h = _rms(x, gpre_ref[...]).astype(BF16)
    acc = None
    for start, size in FFN_CHUNKS:
        g = _dot(h, wg_ref[:, start:start + size])
        u = _dot(h, wu_ref[:, start:start + size])
        a = (g * jax.nn.sigmoid(g) * u).astype(BF16)
        part = _dot(a, wd_ref[start:start + size, :])
        acc = part if acc is None else acc + part
    return x + 0.5 * _rms(acc, gpost_ref[...])


def _ffn_kernel(x_ref, gpre_ref, gpost_ref, wg_ref, wu_ref, wd_ref, o_ref):
    o_ref[...] = _ffn_body(x_ref[...], gpre_ref, gpost_ref, wg_ref, wu_ref, wd_ref)


def _post_ffn_kernel(x_ref, oa_ref, ga_ref, mbm_ref, wdo_ref, wout_ref, gmix_ref,
                     gpre_ref, gpost_ref, wg_ref, wu_ref, wd_ref, o_ref):
    y_a = _dot(oa_ref[...], wdo_ref[...])
    merged = ga_ref[...].astype(F32) * y_a + mbm_ref[...].astype(F32)
    x = x_ref[...] + _rms(_dot(merged.astype(BF16), wout_ref[...]), gmix_ref[...])
    o_ref[...] = _ffn_body(x, gpre_ref, gpost_ref, wg_ref, wu_ref, wd_ref)


def _ffn_weights(g_pre, g_post, w_gu, w_down):
    d = g_pre.shape[-1]
    return [g_pre.reshape(1, d), g_post.reshape(1, d), w_gu[:, :D_FF].astype(BF16),
            w_gu[:, D_FF:].astype(BF16), w_down.astype(BF16)]


def _ffn(x2d, g_pre, g_post, w_gu, w_down):
    n, d = x2d.shape
    assert n % FFN_TM == 0 and d == D_MODEL
    consts = _ffn_weights(g_pre, g_post, w_gu, w_down)
    row = pl.BlockSpec((FFN_TM, d), lambda i: (i, 0))
    return pl.pallas_call(
        _ffn_kernel,
        grid=(n // FFN_TM,),
        in_specs=[row] + [_const_spec(c.shape) for c in consts],
        out_specs=row,
        out_shape=jax.ShapeDtypeStruct((n, d), F32),
        compiler_params=pltpu.CompilerParams(
            dimension_semantics=("arbitrary",), vmem_limit_bytes=VMEM_LIMIT),
        name="ffn",
    )(x2d, *consts)


def _post_ffn(x2d, oa2d, ga2d, mbm2d, p, g_pre, g_post, w_gu, w_down):
    n, d = x2d.shape
    assert n % FFN_TM == 0 and d == D_MODEL
    consts = [p["w_dsa_o"], p["w_out"], p["mix_post_g"]] + _ffn_weights(g_pre, g_post, w_gu, w_down)

    def row(width):
        return pl.BlockSpec((FFN_TM, width), lambda i: (i, 0))

    return pl.pallas_call(
        _post_ffn_kernel,
        grid=(n // FFN_TM,),
        in_specs=[row(d), row(oa2d.shape[1]), row(d), row(d)] + [_const_spec(c.shape) for c in consts],
        out_specs=row(d),
        out_shape=jax.ShapeDtypeStruct((n, d), F32),
        compiler_params=pltpu.CompilerParams(
            dimension_semantics=("arbitrary",), vmem_limit_bytes=VMEM_LIMIT),
        name="post_ffn",
    )(x2d, oa2d, ga2d, mbm2d, *consts)


def _mix_pre_kernel(x_ref, mem_ref, gpre_ref, wcq_ref, wckv_ref, wsm_ref, wglu_ref, wqm_ref,
                    wgate_ref, qng_ref, kvng_ref, ilg_ref, ilb_ref, wdw_ref, bdw_ref, clg_ref,
                    clb_ref, wco_ref, mng_ref, wmkv_ref, wmo_ref,
                    cq_ref, ckv_ref, ckvt_ref, kidx_ref, widxt_ref, ga_ref, mbm_ref,
                    ubuf_ref, ushift_ref, kmem_ref, vmem_ref, gbm_ref, yconv_ref, wtap_ref):
    j = pl.program_id(1)
    tm = x_ref.shape[1]
    hm = N_HEADS_M * D_HEAD_M

    @pl.when(j == 0)
    def _():
        ubuf_ref[0:HALO, :] = jnp.zeros((HALO, CONV_CH), F32)
        for t in range(CONV_WIDTH):
            wtap_ref[t] = jnp.broadcast_to(wdw_ref[t:t + 1, :], (SUBLANES, CONV_CH))
        mk = _rms(mem_ref[0], mng_ref[...]).astype(BF16)
        kv = _dot(mk, wmkv_ref[...])
        kmem_ref[...] = kv[:, :hm].astype(BF16)
        vmem_ref[...] = kv[:, hm:].astype(BF16)

    h = _rms(x_ref[0], gpre_ref[...]).astype(BF16)

    uglu = _dot(h, wglu_ref[...])
    ubuf_ref[HALO:HALO + tm, :] = uglu[:, :CONV_CH] * jax.nn.sigmoid(uglu[:, CONV_CH:])
    span = tm + HALO - SUBLANES
    for r in range(1, SUBLANES):
        ushift_ref[r - 1] = ubuf_ref[r:r + span, :]

    ties = []
    for c0 in range(0, 3 * D_MODEL, GATE_COLS):
        g = jax.nn.sigmoid(_dot(h, wgate_ref[:, c0:c0 + GATE_COLS]))
        if c0 < D_MODEL:
            ga_ref[0, :, c0:c0 + GATE_COLS] = g.astype(BF16)
        else:
            gbm_ref[:, c0 - D_MODEL:c0 - D_MODEL + GATE_COLS] = g
        ties.append(g[0:SUBLANES, 0:LANES] * 0.0)

    first = HALO - (CONV_WIDTH - 1)
    n_blocks = tm // CONV_ROWS
    for blk in range(n_blocks):
        r0 = blk * CONV_ROWS
        y = None
        for t in range(CONV_WIDTH):
            shift, base = (first + t) % SUBLANES, (first + t) // SUBLANES * SUBLANES + r0
            rows = (ubuf_ref[base:base + CONV_ROWS, :] if shift == 0
                    else ushift_ref[shift - 1, base:base + CONV_ROWS, :])
            term = wtap_ref[t][None] * rows.reshape(CONV_ROWS // SUBLANES, SUBLANES, CONV_CH)
            y = term if y is None else y + term
        tie = jnp.tile(ties[blk * len(ties) // n_blocks], (CONV_ROWS // SUBLANES, CONV_CH // LANES))
        yconv_ref[r0:r0 + CONV_ROWS, :] = y.reshape(CONV_ROWS, CONV_CH) + bdw_ref[...] + tie
    ubuf_ref[0:HALO, :] = ubuf_ref[tm:tm + HALO, :]
    z = _layer_norm(yconv_ref[...], clg_ref[...], clb_ref[...])
    y_b = _dot((z * jax.nn.sigmoid(z)).astype(BF16), wco_ref[...])

    cq_ref[0] = _rms(_dot(h, wcq_ref[...]), qng_ref[...]).astype(BF16)
    ckv = _rms(_dot(h, wckv_ref[...]), kvng_ref[...])
    ckv_ref[0] = ckv.astype(BF16)
    for k in range(tm // CK):
        ckvt_ref[0, k, 0:KV_LORA, :] = ckv[k * CK:(k + 1) * CK, :].T.astype(BF16)
        ckvt_ref[0, k, KV_LORA:KV_LORA + BF16_ROWS, :] = jnp.ones((BF16_ROWS, CK), BF16)
    small = _dot(h, wsm_ref[...])
    kidx_ref[0] = _layer_norm(small[:, :D_IDX], ilg_ref[...], ilb_ref[...]).astype(BF16)
    small_t = small.T
    widxt_ref[0] = small_t[D_IDX:D_IDX + SUBLANES, :] * (N_IDX ** -0.5)

    qm = _dot(h, wqm_ref[...]).astype(BF16)
    heads = []
    for hh in range(N_HEADS_M):
        sl = slice(hh * D_HEAD_M, (hh + 1) * D_HEAD_M)
        s = _dot_nt(qm[:, sl], kmem_ref[:, sl]) * (D_HEAD_M ** -0.5)
        p = jnp.exp(s - jnp.max(s, axis=-1, keepdims=True))
        o = _dot(p.astype(BF16), vmem_ref[:, sl])
        heads.append((o / jnp.sum(p, axis=-1, keepdims=True)).astype(BF16))
    y_m = _dot(jnp.concatenate(heads, axis=-1), wmo_ref[...])

    mbm_ref[0] = (gbm_ref[:, :D_MODEL] * y_b + gbm_ref[:, D_MODEL:] * y_m).astype(BF16)


def _mix_pre(x, mem, p):
    b, t, d = x.shape
    tm = min(PRE_TM, t)
    assert t % tm == 0 and tm % CK == 0 and d == D_MODEL and mem.shape == (b, N_MEM, d)
    hm = N_HEADS_M * D_HEAD_M

    def tile(width):
        return pl.BlockSpec((1, tm, width), lambda i, j: (i, j, 0))

    consts = [p["mix_pre_g"], p["w_cq"], p["w_ckv"], p["w_small"], p["w_glu"], p["w_qmem"],
              p["w_gate"], p["q_norm_g"], p["kv_norm_g"], p["idx_ln_g"], p["idx_ln_b"], p["w_dw"],
              p["b_dw"], p["conv_ln_g"], p["conv_ln_b"], p["w_conv_out"], p["mem_norm_g"],
              p["w_mem_kv"], p["w_mem_o"]]
    kvt_rows = KV_LORA + BF16_ROWS
    out_shape = (
        jax.ShapeDtypeStruct((b, t, Q_LORA), BF16),
        jax.ShapeDtypeStruct((b, t, KV_LORA), BF16),
        jax.ShapeDtypeStruct((b, t // CK, kvt_rows, CK), BF16),
        jax.ShapeDtypeStruct((b, t, D_IDX), BF16),
        jax.ShapeDtypeStruct((b, SUBLANES, t), F32),
        jax.ShapeDtypeStruct((b, t, d), BF16),
        jax.ShapeDtypeStruct((b, t, d), BF16),
    )
    out_specs = (
        tile(Q_LORA), tile(KV_LORA),
        pl.BlockSpec((1, tm // CK, kvt_rows, CK), lambda i, j: (i, j, 0, 0)),
        tile(D_IDX),
        pl.BlockSpec((1, SUBLANES, tm), lambda i, j: (i, 0, j)),
        tile(d), tile(d),
    )
    return pl.pallas_call(
        _mix_pre_kernel,
        grid=(b, t // tm),
        in_specs=[tile(d), pl.BlockSpec((1, N_MEM, d), lambda i, j: (i, 0, 0))]
        + [_const_spec(c.shape) for c in consts],
        out_specs=out_specs,
        out_shape=out_shape,
        scratch_shapes=[pltpu.VMEM((HALO + tm, CONV_CH), F32),
                        pltpu.VMEM((SUBLANES - 1, HALO + tm - SUBLANES, CONV_CH), F32),
                        pltpu.VMEM((N_MEM, hm), BF16), pltpu.VMEM((N_MEM, hm), BF16),
                        pltpu.VMEM((tm, 2 * d), F32), pltpu.VMEM((tm, CONV_CH), F32),
                        pltpu.VMEM((CONV_WIDTH, SUBLANES, CONV_CH), F32)],
        compiler_params=pltpu.CompilerParams(
            dimension_semantics=("arbitrary", "arbitrary"), vmem_limit_bytes=VMEM_LIMIT),
        name="mix_pre",
    )(x, mem, *consts)


def _mix_dsa_kernel(topk, cq_ref, widxt_ref, ckv_ref, ckvt_ref, kidx_ref,
                    wiqt_ref, wuqt_ref, wuk_ref, wuvt_ref, ltri_ref,
                    o_ref, sc_ref, acc_ref, qlat_ref, s_ref, lg_ref):
    qi = pl.program_id(1)
    nch = (qi * TQ + TQ + CK - 1) // CK

    cq = cq_ref[0]
    qpos = qi * TQ + lax.broadcasted_iota(I32, (CK, TQ), 1)
    krow = lax.broadcasted_iota(I32, (CK, TQ), 0)

    qidx_t = (_dot_nt(wiqt_ref[...], cq) * (D_IDX ** -0.5)).astype(BF16)
    widx_t = widxt_ref[0]

    q_t = _dot_nt(wuqt_ref[...], cq).astype(BF16)
    for hh in range(N_HEADS_A):
        qlat_ref[hh] = (_dot(wuk_ref[hh], q_t[hh * D_QK:(hh + 1) * D_QK, :])
                        * (D_QK ** -0.5 * LOG2E)).astype(BF16)
    acc_ref[...] = jnp.zeros_like(acc_ref)

    def fold(x, op):
        return op(x.reshape(CK // CNT_ROWS, CNT_ROWS, TQ), axis=0)

    def logits(c, hh):
        start = pl.multiple_of(c * CK, CK)
        lg_ref[hh] = _dot(kidx_ref[0, pl.ds(start, CK), :], qidx_t[hh * D_IDX:(hh + 1) * D_IDX, :])

    def combine(c, carry, next_chunk):
        n_ge0, n_gt0, s_max, s_min = carry
        start = pl.multiple_of(c * CK, CK)
        score = None
        for hh in range(N_IDX):
            term = widx_t[hh:hh + 1, :] * jnp.maximum(lg_ref[hh], 0.0)
            score = term if score is None else score + term
            if next_chunk is not None:
                logits(next_chunk, hh)
        causal = krow + start <= qpos
        score = jnp.where(causal, score + 0.0, NEG_INF)
        sc_ref[pl.ds(start, CK), :] = score
        return (n_ge0 + fold(jnp.where(score >= 0.0, 1, 0), jnp.sum),
                n_gt0 + fold(jnp.where(score > 0.0, 1, 0), jnp.sum),
                jnp.maximum(s_max, fold(score, jnp.max)),
                jnp.minimum(s_min, fold(jnp.where(causal, score, jnp.inf), jnp.min)))

    zeros_i = jnp.zeros((CNT_ROWS, TQ), I32)
    for hh in range(N_IDX):
        logits(0, hh)
    stats = lax.fori_loop(1, nch, lambda c, carry: combine(c - 1, carry, c),
                          (zeros_i, zeros_i, jnp.full((CNT_ROWS, TQ), -jnp.inf, F32),
                           jnp.full((CNT_ROWS, TQ), jnp.inf, F32)))
    stats = combine(nch - 1, stats, None)
    n_ge0 = jnp.sum(stats[0], axis=0, keepdims=True)
    n_gt0 = jnp.sum(stats[1], axis=0, keepdims=True)
    s_max = jnp.max(stats[2], axis=0, keepdims=True)
    s_min = jnp.min(stats[3], axis=0, keepdims=True)
    n_causal = qpos[0:1, :] + 1

    def count_ge(thr):
        def body(c, cnt):
            start = pl.multiple_of(c * CK, CK)
            return cnt + fold(jnp.where(sc_ref[pl.ds(start, CK), :] >= thr, 1, 0), jnp.sum)
        return jnp.sum(lax.fori_loop(0, nch, body, zeros_i), axis=0, keepdims=True)

    few = n_causal < topk
    zero_tie = jnp.where(n_gt0 < topk, jnp.where(n_ge0 >= topk, 1, 0), 0) == 1
    positive = n_gt0 >= topk
    lo = jnp.where(few, NEG_INF, jnp.where(positive, 0.0, jnp.where(zero_tie, 0.0, s_min)))
    hi = jnp.where(positive, jnp.minimum(s_max * (1.0 + 1e-6) + 1e-30, F32_BIG), 0.0)
    c_lo = jnp.where(few, topk, jnp.where(positive, n_ge0, jnp.where(zero_tie, n_ge0, n_causal)))
    c_hi = jnp.where(positive, 0, jnp.where(zero_tie, n_gt0, n_ge0))
    done = jnp.where(few, 1, jnp.where(zero_tie, 1, jnp.where(c_lo == topk, 1, 0)))

    def halve(state):
        lo, hi, c_lo, c_hi, done = state
        mid = lo + (hi - lo) * 0.5
        stuck = jnp.where(mid <= lo, 1, jnp.where(mid >= hi, 1, 0))
        c = count_ge(mid)
        live = jnp.where(done == 0, 1 - stuck, 0)
        raise_lo = jnp.where(c >= topk, live, 0) == 1
        lower_hi = jnp.where(c >= topk, 0, live) == 1
        lo, c_lo = jnp.where(raise_lo, mid, lo), jnp.where(raise_lo, c, c_lo)
        hi, c_hi = jnp.where(lower_hi, mid, hi), jnp.where(lower_hi, c, c_hi)
        return lo, hi, c_lo, c_hi, jnp.where(c_lo == topk, 1, jnp.maximum(done, stuck))

    state = lax.fori_loop(0, SEARCH_UNCHECKED, lambda _, st: halve(st), (lo, hi, c_lo, c_hi, done))

    def checked_halve(carry):
        state = halve(carry[2])
        return carry[0] + 1, jnp.min(state[4]), state

    _, _, (kth, _, n_ge, c_hi, _) = lax.while_loop(
        lambda carry: jnp.logical_and(carry[0] < SEARCH_CAP, carry[1] == 0), checked_halve,
        (jnp.int32(0), jnp.min(state[4]), state))
    n_tie_f = jnp.where(n_ge == topk, topk, topk - c_hi).astype(F32)

    def mask_body(ranked, c, tie_seen):
        start = pl.multiple_of(c * CK, CK)
        sc = sc_ref[pl.ds(start, CK), :]
        if ranked:
            tie = sc == kth
            tie_rank = _dot(ltri_ref[...], jnp.where(tie, 1.0, 0.0).astype(BF16)) + tie_seen
            bias = jnp.where(sc > kth, 0.0,
                             jnp.where(tie, jnp.where(tie_rank <= n_tie_f, 0.0, NEG_INF), NEG_INF))
            tie_seen = tie_rank[CK - 1:CK, :]
        else:
            bias = jnp.where(sc >= kth, 0.0, NEG_INF)
        sc_ref[pl.ds(start, CK), :] = jnp.where(krow + start <= qpos, bias, NEG_INF)
        return tie_seen

    ranked = jnp.max(jnp.where(n_ge > topk, 1, 0)) > 0
    no_ties = jnp.zeros((1, TQ), F32)

    @pl.when(ranked)
    def _():
        lax.fori_loop(0, nch, functools.partial(mask_body, True), no_ties)

    @pl.when(jnp.logical_not(ranked))
    def _():
        lax.fori_loop(0, nch, functools.partial(mask_body, False), no_ties)


    def scores(c, hh):
        start = pl.multiple_of(c * CK, CK)
        s = _dot(ckv_ref[0, pl.ds(start, CK), :], qlat_ref[hh]) + sc_ref[pl.ds(start, CK), :]
        s_ref[hh] = s
        return jnp.max(s, axis=0, keepdims=True)

    def accumulate(c, hh, s_max, m_run, l_run):
        m_new = jnp.maximum(m_run, s_max)
        p = jnp.exp2(s_ref[hh] - jnp.maximum(m_new, 0.1 * NEG_INF)).astype(BF16)
        alpha = jnp.exp2(m_run - m_new)
        pv = _dot(ckvt_ref[0, c], p)
        acc_ref[hh] = alpha * acc_ref[hh] + pv[0:KV_LORA, :]
        return m_new, alpha * l_run + pv[KV_LORA:KV_LORA + 1, :]

    def attn_body(c, carry):
        m_run, l_run, s_max = carry
        m_out, l_out, s_out = [], [], []
        for hh in range(N_HEADS_A):
            m_new, l_new = accumulate(c - 1, hh, s_max[hh], m_run[hh], l_run[hh])
            s_out.append(scores(c, hh))
            m_out.append(m_new)
            l_out.append(l_new)
        return tuple(m_out), tuple(l_out), tuple(s_out)

    heads = range(N_HEADS_A)
    init = (tuple(jnp.full((1, TQ), NEG_INF, F32) for _ in heads),
            tuple(jnp.zeros((1, TQ), F32) for _ in heads),
            tuple(scores(0, hh) for hh in heads))
    m_run, l_run, s_max = lax.fori_loop(1, nch, attn_body, init)
    l_fin = [accumulate(nch - 1, hh, s_max[hh], m_run[hh], l_run[hh])[1] for hh in heads]

    o_t = jnp.concatenate(
        [_dot(wuvt_ref[hh], (acc_ref[hh] * (1.0 / l_fin[hh])).astype(BF16)) for hh in range(N_HEADS_A)],
        axis=0)
    o_ref[0] = o_t.T.astype(BF16)


def _mix_dsa(pre, p, t):
    cq, ckv, ckvt, kidx, widxt = pre
    b = cq.shape[0]
    assert t % CK == 0 and t % TQ == 0
    topk = min(TOPK_MAX, t // 4)
    assert topk <= CK
    ltri = (lax.broadcasted_iota(I32, (CK, CK), 1) <= lax.broadcasted_iota(I32, (CK, CK), 0)).astype(BF16)
    consts = [p["w_idx_q_t"], p["w_uq_t"], p["w_uk_h"], p["w_uv_t"], ltri]

    def whole(shape):
        nd = len(shape)
        return pl.BlockSpec((1,) + tuple(shape[1:]), lambda i, j: (i,) + (0,) * (nd - 1))

    return pl.pallas_call(
        functools.partial(_mix_dsa_kernel, topk),
        grid=(b, t // TQ),
        in_specs=[pl.BlockSpec((1, TQ, Q_LORA), lambda i, j: (i, j, 0)),
                  pl.BlockSpec((1, SUBLANES, TQ), lambda i, j: (i, 0, j)),
                  whole(ckv.shape), whole(ckvt.shape), whole(kidx.shape)]
        + [_const_spec(c.shape) for c in consts],
        out_specs=pl.BlockSpec((1, TQ, N_HEADS_A * D_V), lambda i, j: (i, j, 0)),
        out_shape=jax.ShapeDtypeStruct((b, t, N_HEADS_A * D_V), BF16),
        scratch_shapes=[pltpu.VMEM((t, TQ), F32), pltpu.VMEM((N_HEADS_A, KV_LORA, TQ), F32),
                        pltpu.VMEM((N_HEADS_A, KV_LORA, TQ), BF16), pltpu.VMEM((N_HEADS_A, CK, TQ), F32),
                        pltpu.VMEM((N_IDX, CK, TQ), F32)],
        compiler_params=pltpu.CompilerParams(
            dimension_semantics=("arbitrary", "arbitrary"), vmem_limit_bytes=VMEM_LIMIT),
        name="mix_dsa",
    )(cq, widxt, ckv, ckvt, kidx, *consts)


def _mixer_params(mix_pre_g, mix_post_g, w_in, q_norm_g, kv_norm_g, w_uq, w_uk, w_uv, w_idx_q,
                  idx_ln_g, idx_ln_b, w_dsa_o, w_dw, b_dw, conv_ln_g, conv_ln_b, w_conv_out,
                  mem_norm_g, w_mem_kv, w_mem_o, w_out):
    hm = N_HEADS_M * D_HEAD_M
    cuts = [0]
    for width in (Q_LORA, KV_LORA, D_IDX, N_IDX, 2 * CONV_CH, hm, 3 * D_MODEL):
        cuts.append(cuts[-1] + width)
    assert w_in.shape == (D_MODEL, cuts[-1])
    w_small = jnp.pad(w_in[:, cuts[2]:cuts[4]], ((0, 0), (0, LANES - D_IDX - N_IDX)))
    row = lambda v: v.reshape(1, -1).astype(F32)
    return {
        "mix_pre_g": row(mix_pre_g), "mix_post_g": row(mix_post_g),
        "w_cq": w_in[:, cuts[0]:cuts[1]].astype(BF16), "w_ckv": w_in[:, cuts[1]:cuts[2]].astype(BF16),
        "w_small": w_small.astype(BF16),
        "w_glu": w_in[:, cuts[4]:cuts[5]].astype(BF16), "w_qmem": w_in[:, cuts[5]:cuts[6]].astype(BF16),
        "w_gate": w_in[:, cuts[6]:cuts[7]].astype(BF16),
        "q_norm_g": row(q_norm_g), "kv_norm_g": row(kv_norm_g),
        "idx_ln_g": row(idx_ln_g), "idx_ln_b": row(idx_ln_b),
        "w_dw": w_dw.astype(F32), "b_dw": row(b_dw),
        "conv_ln_g": row(conv_ln_g), "conv_ln_b": row(conv_ln_b),
        "w_conv_out": w_conv_out.astype(BF16), "mem_norm_g": row(mem_norm_g),
        "w_mem_kv": w_mem_kv.astype(BF16), "w_mem_o": w_mem_o.astype(BF16),
        "w_idx_q_t": w_idx_q.T.astype(BF16), "w_uq_t": w_uq.T.astype(BF16),
        "w_uk_h": jnp.transpose(w_uk, (1, 0, 2)).astype(BF16),
        "w_uv_t": jnp.transpose(w_uv, (1, 2, 0)).astype(BF16),
        "w_dsa_o": w_dsa_o.astype(BF16), "w_out": w_out.astype(BF16),
    }


def kernel(x, mem, ffn1_pre_g, ffn1_post_g, ffn1_w_gu, ffn1_w_down, mix_pre_g, mix_post_g, w_in, q_norm_g, kv_norm_g, w_uq, w_uk, w_uv, w_idx_q, idx_ln_g, idx_ln_b, w_dsa_o, w_dw, b_dw, conv_ln_g, conv_ln_b, w_conv_out, mem_norm_g, w_mem_kv, w_mem_o, w_out, ffn2_pre_g, ffn2_post_g, ffn2_w_gu, ffn2_w_down):
    b, t, d = x.shape
    n = b * t
    for l in range(ffn1_pre_g.shape[0]):
        x = _ffn(x.reshape(n, d), ffn1_pre_g[l], ffn1_post_g[l], ffn1_w_gu[l], ffn1_w_down[l])
        p = _mixer_params(mix_pre_g[l], mix_post_g[l], w_in[l], q_norm_g[l], kv_norm_g[l], w_uq[l],
                          w_uk[l], w_uv[l], w_idx_q[l], idx_ln_g[l], idx_ln_b[l], w_dsa_o[l], w_dw[l],
                          b_dw[l], conv_ln_g[l], conv_ln_b[l], w_conv_out[l], mem_norm_g[l],
                          w_mem_kv[l], w_mem_o[l], w_out[l])
        cq, ckv, ckvt, kidx, widxt, ga, mbm = _mix_pre(x.reshape(b, t, d), mem, p)
        oa = _mix_dsa((cq, ckv, ckvt, kidx, widxt), p, t)
        x = _post_ffn(x, oa.reshape(n, -1), ga.reshape(n, d), mbm.reshape(n, d), p,
                      ffn2_pre_g[l], ffn2_post_g[l], ffn2_w_gu[l], ffn2_w_down[l])
        x = x.reshape(b, t, d)
    return x
```

```python
import functools
import math

import jax
import jax.numpy as jnp
from jax import lax
from jax.experimental import pallas as pl
from jax.experimental.pallas import tpu as pltpu

F32 = jnp.float32
BF16 = jnp.bfloat16
I32 = jnp.int32

EPS = 1e-6
NEG_INF = -1e30
D_MODEL = 1024
D_FF = 2816
N_HEADS_A = 8
D_QK = 64
D_V = 64
Q_LORA = 256
KV_LORA = 256
N_IDX = 4
D_IDX = 64
TOPK_MAX = 256
CONV_CH = 512
CONV_WIDTH = 31
N_HEADS_M = 4
D_HEAD_M = 128
N_MEM = 256

LANES = 128
SUBLANES = 8
BF16_ROWS = 16
VMEM_LIMIT = 56 * 1024 * 1024
F32_BIG = 3.0e38
SEARCH_CAP = 288
SEARCH_UNCHECKED = 19
LOG2E = math.log2(math.e)

FFN_TM = 512
FFN_CHUNKS = ((0, 1536), (1536, 1280))
PRE_TM = 512
HALO = 32
GATE_COLS = 256
CONV_ROWS = 32
TQ = 256
CK = 256
CNT_ROWS = 32


def _rms(x, g):
    return x * lax.rsqrt(jnp.mean(x * x, axis=-1, keepdims=True) + EPS) * g


def _layer_norm(x, g, b):
    mu = jnp.mean(x, axis=-1, keepdims=True)
    xc = x - mu
    var = jnp.mean(xc * xc, axis=-1, keepdims=True)
    return xc * lax.rsqrt(var + EPS) * g + b


def _dot(a, b):
    return jnp.dot(a, b, preferred_element_type=F32)


def _dot_nt(a, b):
    return lax.dot_general(a, b, (((1,), (1,)), ((), ())), preferred_element_type=F32)


def _const_spec(shape):
    nd = len(shape)
    return pl.BlockSpec(shape, lambda *_: (0,) * nd, pipeline_mode=pl.Buffered(1))


def _ffn_body(x, gpre_ref, gpost_ref, wg_ref, wu_ref, wd_ref):
    h = _rms(x, gpre_ref[...]).astype(BF16)
    acc = None
    for start, size in FFN_CHUNKS:
        g = _dot(h, wg_ref[:, start:start + size])
        u = _dot(h, wu_ref[:, start:start + size])
        a = (g * jax.nn.sigmoid(g) * u).astype(BF16)
        part = _dot(a, wd_ref[start:start + size, :])
        acc = part if acc is None else acc + part
    return x + 0.5 * _rms(acc, gpost_ref[...])


def _ffn_kernel(x_ref, gpre_ref, gpost_ref, wg_ref, wu_ref, wd_ref, o_ref):
    o_ref[...] = _ffn_body(x_ref[...], gpre_ref, gpost_ref, wg_ref, wu_ref, wd_ref)


def _post_ffn_kernel(x_ref, oa_ref, ga_ref, mbm_ref, wdo_ref, wout_ref, gmix_ref,
                     gpre_ref, gpost_ref, wg_ref, wu_ref, wd_ref, o_ref):
    y_a = _dot(oa_ref[...], wdo_ref[...])
    merged = ga_ref[...].astype(F32) * y_a + mbm_ref[...].astype(F32)
    x = x_ref[...] + _rms(_dot(merged.astype(BF16), wout_ref[...]), gmix_ref[...])
    o_ref[...] = _ffn_body(x, gpre_ref, gpost_ref, wg_ref, wu_ref, wd_ref)


def _ffn_weights(g_pre, g_post, w_gu, w_down):
    d = g_pre.shape[-1]
    return [g_pre.reshape(1, d), g_post.reshape(1, d), w_gu[:, :D_FF].astype(BF16),
            w_gu[:, D_FF:].astype(BF16), w_down.astype(BF16)]


def _ffn(x2d, g_pre, g_post, w_gu, w_down):
    n, d = x2d.shape
    assert n % FFN_TM == 0 and d == D_MODEL
    consts = _ffn_weights(g_pre, g_post, w_gu, w_down)
    row = pl.BlockSpec((FFN_TM, d), lambda i: (i, 0))
    return pl.pallas_call(
        _ffn_kernel,
        grid=(n // FFN_TM,),
        in_specs=[row] + [_const_spec(c.shape) for c in consts],
        out_specs=row,
        out_shape=jax.ShapeDtypeStruct((n, d), F32),
        compiler_params=pltpu.CompilerParams(
            dimension_semantics=("arbitrary",), vmem_limit_bytes=VMEM_LIMIT),
        name="ffn",
    )(x2d, *consts)


def _post_ffn(x2d, oa2d, ga2d, mbm2d, p, g_pre, g_post, w_gu, w_down):
    n, d = x2d.shape
    assert n % FFN_TM == 0 and d == D_MODEL
    consts = [p["w_dsa_o"], p["w_out"], p["mix_post_g"]] + _ffn_weights(g_pre, g_post, w_gu, w_down)

    def row(width):
        return pl.BlockSpec((FFN_TM, width), lambda i: (i, 0))

    return pl.pallas_call(
        _post_ffn_kernel,
        grid=(n // FFN_TM,),
        in_specs=[row(d), row(oa2d.shape[1]), row(d), row(d)] + [_const_spec(c.shape) for c in consts],
        out_specs=row(d),
        out_shape=jax.ShapeDtypeStruct((n, d), F32),
        compiler_params=pltpu.CompilerParams(
            dimension_semantics=("arbitrary",), vmem_limit_bytes=VMEM_LIMIT),
        name="post_ffn",
    )(x2d, oa2d, ga2d, mbm2d, *consts)


def _mix_pre_kernel(x_ref, mem_ref, gpre_ref, wcq_ref, wckv_ref, wsm_ref, wglu_ref, wqm_ref,
                    wgate_ref, qng_ref, kvng_ref, ilg_ref, ilb_ref, wdw_ref, bdw_ref, clg_ref,
                    clb_ref, wco_ref, mng_ref, wmkv_ref, wmo_ref,
                    cq_ref, ckv_ref, ckvt_ref, kidx_ref, widxt_ref, ga_ref, mbm_ref,
                    ubuf_ref, ushift_ref, kmem_ref, vmem_ref, gbm_ref, yconv_ref, wtap_ref):
    j = pl.program_id(1)
    tm = x_ref.shape[1]
    hm = N_HEADS_M * D_HEAD_M

    @pl.when(j == 0)
    def _():
        ubuf_ref[0:HALO, :] = jnp.zeros((HALO, CONV_CH), F32)
        for t in range(CONV_WIDTH):
            wtap_ref[t] = jnp.broadcast_to(wdw_ref[t:t + 1, :], (SUBLANES, CONV_CH))
        mk = _rms(mem_ref[0], mng_ref[...]).astype(BF16)
        kv = _dot(mk, wmkv_ref[...])
        kmem_ref[...] = kv[:, :hm].astype(BF16)
        vmem_ref[...] = kv[:, hm:].astype(BF16)

    h = _rms(x_ref[0], gpre_ref[...]).astype(BF16)

    uglu = _dot(h, wglu_ref[...])
    ubuf_ref[HALO:HALO + tm, :] = uglu[:, :CONV_CH] * jax.nn.sigmoid(uglu[:, CONV_CH:])
    span = tm + HALO - SUBLANES
    for r in range(1, SUBLANES):
        ushift_ref[r - 1] = ubuf_ref[r:r + span, :]

    ties = []
    for c0 in range(0, 3 * D_MODEL, GATE_COLS):
        g = jax.nn.sigmoid(_dot(h, wgate_ref[:, c0:c0 + GATE_COLS]))
        if c0 < D_MODEL:
            ga_ref[0, :, c0:c0 + GATE_COLS] = g.astype(BF16)
        else:
            gbm_ref[:, c0 - D_MODEL:c0 - D_MODEL + GATE_COLS] = g
        ties.append(g[0:SUBLANES, 0:LANES] * 0.0)

    first = HALO - (CONV_WIDTH - 1)
    n_blocks = tm // CONV_ROWS
    for blk in range(n_blocks):
        r0 = blk * CONV_ROWS
        y = None
        for t in range(CONV_WIDTH):
            shift, base = (first + t) % SUBLANES, (first + t) // SUBLANES * SUBLANES + r0
            rows = (ubuf_ref[base:base + CONV_ROWS, :] if shift == 0
                    else ushift_ref[shift - 1, base:base + CONV_ROWS, :])
            term = wtap_ref[t][None] * rows.reshape(CONV_ROWS // SUBLANES, SUBLANES, CONV_CH)
            y = term if y is None else y + term
        tie = jnp.tile(ties[blk * len(ties) // n_blocks], (CONV_ROWS // SUBLANES, CONV_CH // LANES))
        yconv_ref[r0:r0 + CONV_ROWS, :] = y.reshape(CONV_ROWS, CONV_CH) + bdw_ref[...] + tie
    ubuf_ref[0:HALO, :] = ubuf_ref[tm:tm + HALO, :]
    z = _layer_norm(yconv_ref[...], clg_ref[...], clb_ref[...])
    y_b = _dot((z * jax.nn.sigmoid(z)).astype(BF16), wco_ref[...])

    cq_ref[0] = _rms(_dot(h, wcq_ref[...]), qng_ref[...]).astype(BF16)
    ckv = _rms(_dot(h, wckv_ref[...]), kvng_ref[...])
    ckv_ref[0] = ckv.astype(BF16)
    for k in range(tm // CK):
        ckvt_ref[0, k, 0:KV_LORA, :] = ckv[k * CK:(k + 1) * CK, :].T.astype(BF16)
        ckvt_ref[0, k, KV_LORA:KV_LORA + BF16_ROWS, :] = jnp.ones((BF16_ROWS, CK), BF16)
    small = _dot(h, wsm_ref[...])
    kidx_ref[0] = _layer_norm(small[:, :D_IDX], ilg_ref[...], ilb_ref[...]).astype(BF16)
    small_t = small.T
    widxt_ref[0] = small_t[D_IDX:D_IDX + SUBLANES, :] * (N_IDX ** -0.5)

    qm = _dot(h, wqm_ref[...]).astype(BF16)
    heads = []
    for hh in range(N_HEADS_M):
        sl = slice(hh * D_HEAD_M, (hh + 1) * D_HEAD_M)
        s = _dot_nt(qm[:, sl], kmem_ref[:, sl]) * (D_HEAD_M ** -0.5)
        p = jnp.exp(s - jnp.max(s, axis=-1, keepdims=True))
        o = _dot(p.astype(BF16), vmem_ref[:, sl])
        heads.append((o / jnp.sum(p, axis=-1, keepdims=True)).astype(BF16))
    y_m = _dot(jnp.concatenate(heads, axis=-1), wmo_ref[...])

    mbm_ref[0] = (gbm_ref[:, :D_MODEL] * y_b + gbm_ref[:, D_MODEL:] * y_m).astype(BF16)


def _mix_pre(x, mem, p):
    b, t, d = x.shape
    tm = min(PRE_TM, t)
    assert t % tm == 0 and tm % CK == 0 and d == D_MODEL and mem.shape == (b, N_MEM, d)
    hm = N_HEADS_M * D_HEAD_M

    def tile(width):
        return pl.BlockSpec((1, tm, width), lambda i, j: (i, j, 0))

    consts = [p["mix_pre_g"], p["w_cq"], p["w_ckv"], p["w_small"], p["w_glu"], p["w_qmem"],
              p["w_gate"], p["q_norm_g"], p["kv_norm_g"], p["idx_ln_g"], p["idx_ln_b"], p["w_dw"],
              p["b_dw"], p["conv_ln_g"], p["conv_ln_b"], p["w_conv_out"], p["mem_norm_g"],
              p["w_mem_kv"], p["w_mem_o"]]
    kvt_rows = KV_LORA + BF16_ROWS
    out_shape = (
        jax.ShapeDtypeStruct((b, t, Q_LORA), BF16),
        jax.ShapeDtypeStruct((b, t, KV_LORA), BF16),
        jax.ShapeDtypeStruct((b, t // CK, kvt_rows, CK), BF16),
        jax.ShapeDtypeStruct((b, t, D_IDX), BF16),
        jax.ShapeDtypeStruct((b, SUBLANES, t), F32),
        jax.ShapeDtypeStruct((b, t, d), BF16),
        jax.ShapeDtypeStruct((b, t, d), BF16),
    )
    out_specs = (
        tile(Q_LORA), tile(KV_LORA),
        pl.BlockSpec((1, tm // CK, kvt_rows, CK), lambda i, j: (i, j, 0, 0)),
        tile(D_IDX),
        pl.BlockSpec((1, SUBLANES, tm), lambda i, j: (i, 0, j)),
        tile(d), tile(d),
    )
    return pl.pallas_call(
        _mix_pre_kernel,
        grid=(b, t // tm),
        in_specs=[tile(d), pl.BlockSpec((1, N_MEM, d), lambda i, j: (i, 0, 0))]
        + [_const_spec(c.shape) for c in consts],
        out_specs=out_specs,
        out_shape=out_shape,
        scratch_shapes=[pltpu.VMEM((HALO + tm, CONV_CH), F32),
                        pltpu.VMEM((SUBLANES - 1, HALO + tm - SUBLANES, CONV_CH), F32),
                        pltpu.VMEM((N_MEM, hm), BF16), pltpu.VMEM((N_MEM, hm), BF16),
                        pltpu.VMEM((tm, 2 * d), F32), pltpu.VMEM((tm, CONV_CH), F32),
                        pltpu.VMEM((CONV_WIDTH, SUBLANES, CONV_CH), F32)],
        compiler_params=pltpu.CompilerParams(
            dimension_semantics=("arbitrary", "arbitrary"), vmem_limit_bytes=VMEM_LIMIT),
        name="mix_pre",
    )(x, mem, *consts)


def _mix_dsa_kernel(topk, cq_ref, widxt_ref, ckv_ref, ckvt_ref, kidx_ref,
                    wiqt_ref, wuqt_ref, wuk_ref, wuvt_ref, ltri_ref,
                    o_ref, sc_ref, acc_ref, qlat_ref, s_ref, lg_ref):
    qi = pl.program_id(1)
    nch = (qi * TQ + TQ + CK - 1) // CK

    cq = cq_ref[0]
    qpos = qi * TQ + lax.broadcasted_iota(I32, (CK, TQ), 1)
    krow = lax.broadcasted_iota(I32, (CK, TQ), 0)

    qidx_t = (_dot_nt(wiqt_ref[...], cq) * (D_IDX ** -0.5)).astype(BF16)
    widx_t = widxt_ref[0]

    q_t = _dot_nt(wuqt_ref[...], cq).astype(BF16)
    for hh in range(N_HEADS_A):
        qlat_ref[hh] = (_dot(wuk_ref[hh], q_t[hh * D_QK:(hh + 1) * D_QK, :])
                        * (D_QK ** -0.5 * LOG2E)).astype(BF16)
    acc_ref[...] = jnp.zeros_like(acc_ref)

    def fold(x, op):
        return op(x.reshape(CK // CNT_ROWS, CNT_ROWS, TQ), axis=0)

    def logits(c, hh):
        start = pl.multiple_of(c * CK, CK)
        lg_ref[hh] = _dot(kidx_ref[0, pl.ds(start, CK), :], qidx_t[hh * D_IDX:(hh + 1) * D_IDX, :])

    def combine(c, carry, next_chunk):
        n_ge0, n_gt0, s_max, s_min = carry
        start = pl.multiple_of(c * CK, CK)
        score = None
        for hh in range(N_IDX):
            term = widx_t[hh:hh + 1, :] * jnp.maximum(lg_ref[hh], 0.0)
            score = term if score is None else score + term
            if next_chunk is not None:
                logits(next_chunk, hh)
        causal = krow + start <= qpos
        score = jnp.where(causal, score + 0.0, NEG_INF)
        sc_ref[pl.ds(start, CK), :] = score
        return (n_ge0 + fold(jnp.where(score >= 0.0, 1, 0), jnp.sum),
                n_gt0 + fold(jnp.where(score > 0.0, 1, 0), jnp.sum),
                jnp.maximum(s_max, fold(score, jnp.max)),
                jnp.minimum(s_min, fold(jnp.where(causal, score, jnp.inf), jnp.min)))

    zeros_i = jnp.zeros((CNT_ROWS, TQ), I32)
    for hh in range(N_IDX):
        logits(0, hh)
    stats = lax.fori_loop(1, nch, lambda c, carry: combine(c - 1, carry, c),
                          (zeros_i, zeros_i, jnp.full((CNT_ROWS, TQ), -jnp.inf, F32),
                           jnp.full((CNT_ROWS, TQ), jnp.inf, F32)))
    stats = combine(nch - 1, stats, None)
    n_ge0 = jnp.sum(stats[0], axis=0, keepdims=True)
    n_gt0 = jnp.sum(stats[1], axis=0, keepdims=True)
    s_max = jnp.max(stats[2], axis=0, keepdims=True)
    s_min = jnp.min(stats[3], axis=0, keepdims=True)
    n_causal = qpos[0:1, :] + 1

    def count_ge(thr):
        def body(c, cnt):
            start = pl.multiple_of(c * CK, CK)
            return cnt + fold(jnp.where(sc_ref[pl.ds(start, CK), :] >= thr, 1, 0), jnp.sum)
        return jnp.sum(lax.fori_loop(0, nch, body, zeros_i), axis=0, keepdims=True)

    few = n_causal < topk
    zero_tie = jnp.where(n_gt0 < topk, jnp.where(n_ge0 >= topk, 1, 0), 0) == 1
    positive = n_gt0 >= topk
    lo = jnp.where(few, NEG_INF, jnp.where(positive, 0.0, jnp.where(zero_tie, 0.0, s_min)))
    hi = jnp.where(positive, jnp.minimum(s_max * (1.0 + 1e-6) + 1e-30, F32_BIG), 0.0)
    c_lo = jnp.where(few, topk, jnp.where(positive, n_ge0, jnp.where(zero_tie, n_ge0, n_causal)))
    c_hi = jnp.where(positive, 0, jnp.where(zero_tie, n_gt0, n_ge0))
    done = jnp.where(few, 1, jnp.where(zero_tie, 1, jnp.where(c_lo == topk, 1, 0)))

    def halve(state):
        lo, hi, c_lo, c_hi, done = state
        mid = lo + (hi - lo) * 0.5
        stuck = jnp.where(mid <= lo, 1, jnp.where(mid >= hi, 1, 0))
        c = count_ge(mid)
        live = jnp.where(done == 0, 1 - stuck, 0)
        raise_lo = jnp.where(c >= topk, live, 0) == 1
        lower_hi = jnp.where(c >= topk, 0, live) == 1
        lo, c_lo = jnp.where(raise_lo, mid, lo), jnp.where(raise_lo, c, c_lo)
        hi, c_hi = jnp.where(lower_hi, mid, hi), jnp.where(lower_hi, c, c_hi)
        return lo, hi, c_lo, c_hi, jnp.where(c_lo == topk, 1, jnp.maximum(done, stuck))

    state = lax.fori_loop(0, SEARCH_UNCHECKED, lambda _, st: halve(st), (lo, hi, c_lo, c_hi, done))

    def checked_halve(carry):
        state = halve(carry[2])
        return carry[0] + 1, jnp.min(state[4]), state

    _, _, (kth, _, n_ge, c_hi, _) = lax.while_loop(
        lambda carry: jnp.logical_and(carry[0] < SEARCH_CAP, carry[1] == 0), checked_halve,
        (jnp.int32(0), jnp.min(state[4]), state))
    n_tie_f = jnp.where(n_ge == topk, topk, topk - c_hi).astype(F32)

    def tie_prefix(c):
        start = pl.multiple_of(c * CK, CK)
        lg_ref[0] = _dot(ltri_ref[...], jnp.where(sc_ref[pl.ds(start, CK), :] == kth, 1.0, 0.0).astype(BF16))

    def ranked_mask(c, tie_seen, next_chunk):
        start = pl.multiple_of(c * CK, CK)
        sc = sc_ref[pl.ds(start, CK), :]
        tie_rank = lg_ref[0] + tie_seen
        if next_chunk is not None:
            tie_prefix(next_chunk)
        bias = jnp.where(sc > kth, 0.0,
                         jnp.where(sc == kth, jnp.where(tie_rank <= n_tie_f, 0.0, NEG_INF), NEG_INF))
        sc_ref[pl.ds(start, CK), :] = jnp.where(krow + start <= qpos, bias, NEG_INF)
        return tie_rank[CK - 1:CK, :]

    tie_prefix(0)
    seen = lax.fori_loop(1, nch, lambda c, seen: ranked_mask(c - 1, seen, c), jnp.zeros((1, TQ), F32))
    ranked_mask(nch - 1, seen, None)


    def scores(c, hh):
        start = pl.multiple_of(c * CK, CK)
        s = _dot(ckv_ref[0, pl.ds(start, CK), :], qlat_ref[hh]) + sc_ref[pl.ds(start, CK), :]
        s_ref[hh] = s
        return jnp.max(s, axis=0, keepdims=True)

    def accumulate(c, hh, s_max, m_run, l_run):
        m_new = jnp.maximum(m_run, s_max)
        p = jnp.exp2(s_ref[hh] - jnp.maximum(m_new, 0.1 * NEG_INF)).astype(BF16)
        alpha = jnp.exp2(m_run - m_new)
        pv = _dot(ckvt_ref[0, c], p)
        acc_ref[hh] = alpha * acc_ref[hh] + pv[0:KV_LORA, :]
        return m_new, alpha * l_run + pv[KV_LORA:KV_LORA + 1, :]

    def attn_body(c, carry):
        m_run, l_run, s_max = carry
        m_out, l_out, s_out = [], [], []
        for hh in range(N_HEADS_A):
            m_new, l_new = accumulate(c - 1, hh, s_max[hh], m_run[hh], l_run[hh])
            s_out.append(scores(c, hh))
            m_out.append(m_new)
            l_out.append(l_new)
        return tuple(m_out), tuple(l_out), tuple(s_out)

    heads = range(N_HEADS_A)
    init = (tuple(jnp.full((1, TQ), NEG_INF, F32) for _ in heads),
            tuple(jnp.zeros((1, TQ), F32) for _ in heads),
            tuple(scores(0, hh) for hh in heads))
    m_run, l_run, s_max = lax.fori_loop(1, nch, attn_body, init)
    l_fin = [accumulate(nch - 1, hh, s_max[hh], m_run[hh], l_run[hh])[1] for hh in heads]

    o_t = jnp.concatenate(
        [_dot(wuvt_ref[hh], (acc_ref[hh] * (1.0 / l_fin[hh])).astype(BF16)) for hh in range(N_HEADS_A)],
        axis=0)
    o_ref[0] = o_t.T.astype(BF16)


def _mix_dsa(pre, p, t):
    cq, ckv, ckvt, kidx, widxt = pre
    b = cq.shape[0]
    assert t % CK == 0 and t % TQ == 0
    topk = min(TOPK_MAX, t // 4)
    assert topk <= CK
    ltri = (lax.broadcasted_iota(I32, (CK, CK), 1) <= lax.broadcasted_iota(I32, (CK, CK), 0)).astype(BF16)
    consts = [p["w_idx_q_t"], p["w_uq_t"], p["w_uk_h"], p["w_uv_t"], ltri]

    def whole(shape):
        nd = len(shape)
        return pl.BlockSpec((1,) + tuple(shape[1:]), lambda i, j: (i,) + (0,) * (nd - 1))

    return pl.pallas_call(
        functools.partial(_mix_dsa_kernel, topk),
        grid=(b, t // TQ),
        in_specs=[pl.BlockSpec((1, TQ, Q_LORA), lambda i, j: (i, j, 0)),
                  pl.BlockSpec((1, SUBLANES, TQ), lambda i, j: (i, 0, j)),
                  whole(ckv.shape), whole(ckvt.shape), whole(kidx.shape)]
        + [_const_spec(c.shape) for c in consts],
        out_specs=pl.BlockSpec((1, TQ, N_HEADS_A * D_V), lambda i, j: (i, j, 0)),
        out_shape=jax.ShapeDtypeStruct((b, t, N_HEADS_A * D_V), BF16),
        scratch_shapes=[pltpu.VMEM((t, TQ), F32), pltpu.VMEM((N_HEADS_A, KV_LORA, TQ), F32),
                        pltpu.VMEM((N_HEADS_A, KV_LORA, TQ), BF16), pltpu.VMEM((N_HEADS_A, CK, TQ), F32),
                        pltpu.VMEM((N_IDX, CK, TQ), F32)],
        compiler_params=pltpu.CompilerParams(
            dimension_semantics=("arbitrary", "arbitrary"), vmem_limit_bytes=VMEM_LIMIT),
        name="mix_dsa",
    )(cq, widxt, ckv, ckvt, kidx, *consts)


def _mixer_params(mix_pre_g, mix_post_g, w_in, q_norm_g, kv_norm_g, w_uq, w_uk, w_uv, w_idx_q,
                  idx_ln_g, idx_ln_b, w_dsa_o, w_dw, b_dw, conv_ln_g, conv_ln_b, w_conv_out,
                  mem_norm_g, w_mem_kv, w_mem_o, w_out):
    hm = N_HEADS_M * D_HEAD_M
    cuts = [0]
    for width in (Q_LORA, KV_LORA, D_IDX, N_IDX, 2 * CONV_CH, hm, 3 * D_MODEL):
        cuts.append(cuts[-1] + width)
    assert w_in.shape == (D_MODEL, cuts[-1])
    w_small = jnp.pad(w_in[:, cuts[2]:cuts[4]], ((0, 0), (0, LANES - D_IDX - N_IDX)))
    row = lambda v: v.reshape(1, -1).astype(F32)
    return {
        "mix_pre_g": row(mix_pre_g), "mix_post_g": row(mix_post_g),
        "w_cq": w_in[:, cuts[0]:cuts[1]].astype(BF16), "w_ckv": w_in[:, cuts[1]:cuts[2]].astype(BF16),
        "w_small": w_small.astype(BF16),
        "w_glu": w_in[:, cuts[4]:cuts[5]].astype(BF16), "w_qmem": w_in[:, cuts[5]:cuts[6]].astype(BF16),
        "w_gate": w_in[:, cuts[6]:cuts[7]].astype(BF16),
        "q_norm_g": row(q_norm_g), "kv_norm_g": row(kv_norm_g),
        "idx_ln_g": row(idx_ln_g), "idx_ln_b": row(idx_ln_b),
        "w_dw": w_dw.astype(F32), "b_dw": row(b_dw),
        "conv_ln_g": row(conv_ln_g), "conv_ln_b": row(conv_ln_b),
        "w_conv_out": w_conv_out.astype(BF16), "mem_norm_g": row(mem_norm_g),
        "w_mem_kv": w_mem_kv.astype(BF16), "w_mem_o": w_mem_o.astype(BF16),
        "w_idx_q_t": w_idx_q.T.astype(BF16), "w_uq_t": w_uq.T.astype(BF16),
        "w_uk_h": jnp.transpose(w_uk, (1, 0, 2)).astype(BF16),
        "w_uv_t": jnp.transpose(w_uv, (1, 2, 0)).astype(BF16),
        "w_dsa_o": w_dsa_o.astype(BF16), "w_out": w_out.astype(BF16),
    }


def kernel(x, mem, ffn1_pre_g, ffn1_post_g, ffn1_w_gu, ffn1_w_down, mix_pre_g, mix_post_g, w_in, q_norm_g, kv_norm_g, w_uq, w_uk, w_uv, w_idx_q, idx_ln_g, idx_ln_b, w_dsa_o, w_dw, b_dw, conv_ln_g, conv_ln_b, w_conv_out, mem_norm_g, w_mem_kv, w_mem_o, w_out, ffn2_pre_g, ffn2_post_g, ffn2_w_gu, ffn2_w_down):
    b, t, d = x.shape
    n = b * t
    for l in range(ffn1_pre_g.shape[0]):
        x = _ffn(x.reshape(n, d), ffn1_pre_g[l], ffn1_post_g[l], ffn1_w_gu[l], ffn1_w_down[l])
        p = _mixer_params(mix_pre_g[l], mix_post_g[l], w_in[l], q_norm_g[l], kv_norm_g[l], w_uq[l],
                          w_uk[l], w_uv[l], w_idx_q[l], idx_ln_g[l], idx_ln_b[l], w_dsa_o[l], w_dw[l],
                          b_dw[l], conv_ln_g[l], conv_ln_b[l], w_conv_out[l], mem_norm_g[l],
                          w_mem_kv[l], w_mem_o[l], w_out[l])
        cq, ckv, ckvt, kidx, widxt, ga, mbm = _mix_pre(x.reshape(b, t, d), mem, p)
        oa = _mix_dsa((cq, ckv, ckvt, kidx, widxt), p, t)
        x = _post_ffn(x, oa.reshape(n, -1), ga.reshape(n, d), mbm.reshape(n, d), p,
                      ffn2_pre_g[l], ffn2_post_g[l], ffn2_w_gu[l], ffn2_w_down[l])
        x = x.reshape(b, t, d)
    return x
```

```python
import functools
import math

import jax
import jax.numpy as jnp
from jax import lax
from jax.experimental import pallas as pl
from jax.experimental.pallas import tpu as pltpu

F32 = jnp.float32
BF16 = jnp.bfloat16
I32 = jnp.int32

EPS = 1e-6
NEG_INF = -1e30
D_MODEL = 1024
D_FF = 2816
N_HEADS_A = 8
D_QK = 64
D_V = 64
Q_LORA = 256
KV_LORA = 256
N_IDX = 4
D_IDX = 64
TOPK_MAX = 256
CONV_CH = 512
CONV_WIDTH = 31
N_HEADS_M = 4
D_HEAD_M = 128
N_MEM = 256

LANES = 128
SUBLANES = 8
BF16_ROWS = 16
VMEM_LIMIT = 56 * 1024 * 1024
F32_BIG = 3.0e38
SEARCH_CAP = 288
SEARCH_UNCHECKED = 19
LOG2E = math.log2(math.e)

FFN_TM = 512
FFN_CHUNKS = ((0, 1536), (1536, 1280))
PRE_TM = 512
HALO = 32
GATE_COLS = 256
CONV_ROWS = 16
TQ = 256
CK = 256
CNT_ROWS = 32


def _rms(x, g):
    return x * lax.rsqrt(jnp.mean(x * x, axis=-1, keepdims=True) + EPS) * g


def _layer_norm(x, g, b):
    mu = jnp.mean(x, axis=-1, keepdims=True)
    xc = x - mu
    var = jnp.mean(xc * xc, axis=-1, keepdims=True)
    return xc * lax.rsqrt(var + EPS) * g + b


def _dot(a, b):
    return jnp.dot(a, b, preferred_element_type=F32)


def _dot_nt(a, b):
    return lax.dot_general(a, b, (((1,), (1,)), ((), ())), preferred_element_type=F32)


def _const_spec(shape):
    nd = len(shape)
    return pl.BlockSpec(shape, lambda *_: (0,) * nd, pipeline_mode=pl.Buffered(1))


def _ffn_body(x, gpre_ref, gpost_ref, wg_ref, wu_ref, wd_ref):
    h = _rms(x, gpre_ref[...]).astype(BF16)
    acc = None
    for start, size in FFN_CHUNKS:
        g = _dot(h, wg_ref[:, start:start + size])
        u = _dot(h, wu_ref[:, start:start + size])
        a = (g * jax.nn.sigmoid(g) * u).astype(BF16)
        part = _dot(a, wd_ref[start:start + size, :])
        acc = part if acc is None else acc + part
    return x + 0.5 * _rms(acc, gpost_ref[...])


def _ffn_kernel(x_ref, gpre_ref, gpost_ref, wg_ref, wu_ref, wd_ref, o_ref):
    o_ref[...] = _ffn_body(x_ref[...], gpre_ref, gpost_ref, wg_ref, wu_ref, wd_ref)


def _post_ffn_kernel(x_ref, oa_ref, ga_ref, mbm_ref, wdo_ref, wout_ref, gmix_ref,
                     gpre_ref, gpost_ref, wg_ref, wu_ref, wd_ref, o_ref):
    y_a = _dot(oa_ref[...], wdo_ref[...])
    merged = ga_ref[...].astype(F32) * y_a + mbm_ref[...].astype(F32)
    x = x_ref[...] + _rms(_dot(merged.astype(BF16), wout_ref[...]), gmix_ref[...])
    o_ref[...] = _ffn_body(x, gpre_ref, gpost_ref, wg_ref, wu_ref, wd_ref)


def _ffn_weights(g_pre, g_post, w_gu, w_down):
    d = g_pre.shape[-1]
    return [g_pre.reshape(1, d), g_post.reshape(1, d), w_gu[:, :D_FF].astype(BF16),
            w_gu[:, D_FF:].astype(BF16), w_down.astype(BF16)]


def _ffn(x2d, g_pre, g_post, w_gu, w_down):
    n, d = x2d.shape
    assert n % FFN_TM == 0 and d == D_MODEL
    consts = _ffn_weights(g_pre, g_post, w_gu, w_down)
    row = pl.BlockSpec((FFN_TM, d), lambda i: (i, 0))
    return pl.pallas_call(
        _ffn_kernel,
        grid=(n // FFN_TM,),
        in_specs=[row] + [_const_spec(c.shape) for c in consts],
        out_specs=row,
        out_shape=jax.ShapeDtypeStruct((n, d), F32),
        compiler_params=pltpu.CompilerParams(
            dimension_semantics=("arbitrary",), vmem_limit_bytes=VMEM_LIMIT),
        name="ffn",
    )(x2d, *consts)


def _post_ffn(x2d, oa2d, ga2d, mbm2d, p, g_pre, g_post, w_gu, w_down):
    n, d = x2d.shape
    assert n % FFN_TM == 0 and d == D_MODEL
    consts = [p["w_dsa_o"], p["w_out"], p["mix_post_g"]] + _ffn_weights(g_pre, g_post, w_gu, w_down)

    def row(width):
        return pl.BlockSpec((FFN_TM, width), lambda i: (i, 0))

    return pl.pallas_call(
        _post_ffn_kernel,
        grid=(n // FFN_TM,),
        in_specs=[row(d), row(oa2d.shape[1]), row(d), row(d)] + [_const_spec(c.shape) for c in consts],
        out_specs=row(d),
        out_shape=jax.ShapeDtypeStruct((n, d), F32),
        compiler_params=pltpu.CompilerParams(
            dimension_semantics=("arbitrary",), vmem_limit_bytes=VMEM_LIMIT),
        name="post_ffn",
    )(x2d, oa2d, ga2d, mbm2d, *consts)


def _mix_pre_kernel(x_ref, mem_ref, gpre_ref, wcq_ref, wckv_ref, wsm_ref, wglu_ref, wqm_ref,
                    wgate_ref, qng_ref, kvng_ref, ilg_ref, ilb_ref, wdw_ref, bdw_ref, clg_ref,
                    clb_ref, wco_ref, mng_ref, wmkv_ref, wmo_ref,
                    cq_ref, ckv_ref, ckvt_ref, kidx_ref, widxt_ref, ga_ref, mbm_ref,
                    ubuf_ref, ushift_ref, kmem_ref, vmem_ref, gbm_ref, yconv_ref, wtap_ref):
    j = pl.program_id(1)
    tm = x_ref.shape[1]
    hm = N_HEADS_M * D_HEAD_M

    @pl.when(j == 0)
    def _():
        ubuf_ref[0:HALO, :] = jnp.zeros((HALO, CONV_CH), F32)
        for t in range(CONV_WIDTH):
            wtap_ref[t] = jnp.broadcast_to(wdw_ref[t:t + 1, :], (SUBLANES, CONV_CH))
        mk = _rms(mem_ref[0], mng_ref[...]).astype(BF16)
        kv = _dot(mk, wmkv_ref[...])
        kmem_ref[...] = kv[:, :hm].astype(BF16)
        vmem_ref[...] = kv[:, hm:].astype(BF16)

    h = _rms(x_ref[0], gpre_ref[...]).astype(BF16)

    uglu = _dot(h, wglu_ref[...])
    ubuf_ref[HALO:HALO + tm, :] = uglu[:, :CONV_CH] * jax.nn.sigmoid(uglu[:, CONV_CH:])
    span = tm + HALO - SUBLANES
    for r in range(1, SUBLANES):
        ushift_ref[r - 1] = ubuf_ref[r:r + span, :]

    ties = []
    for c0 in range(0, 3 * D_MODEL, GATE_COLS):
        g = jax.nn.sigmoid(_dot(h, wgate_ref[:, c0:c0 + GATE_COLS]))
        if c0 < D_MODEL:
            ga_ref[0, :, c0:c0 + GATE_COLS] = g.astype(BF16)
        else:
            gbm_ref[:, c0 - D_MODEL:c0 - D_MODEL + GATE_COLS] = g
        ties.append(g[0:SUBLANES, 0:LANES] * 0.0)

    first = HALO - (CONV_WIDTH - 1)
    n_blocks = tm // CONV_ROWS
    for blk in range(n_blocks):
        r0 = blk * CONV_ROWS
        y = None
        for t in range(CONV_WIDTH):
            shift, base = (first + t) % SUBLANES, (first + t) // SUBLANES * SUBLANES + r0
            rows = (ubuf_ref[base:base + CONV_ROWS, :] if shift == 0
                    else ushift_ref[shift - 1, base:base + CONV_ROWS, :])
            term = wtap_ref[t][None] * rows.reshape(CONV_ROWS // SUBLANES, SUBLANES, CONV_CH)
            y = term if y is None else y + term
        tie = jnp.tile(ties[blk * len(ties) // n_blocks], (CONV_ROWS // SUBLANES, CONV_CH // LANES))
        yconv_ref[r0:r0 + CONV_ROWS, :] = y.reshape(CONV_ROWS, CONV_CH) + bdw_ref[...] + tie
    ubuf_ref[0:HALO, :] = ubuf_ref[tm:tm + HALO, :]
    z = _layer_norm(yconv_ref[...], clg_ref[...], clb_ref[...])
    y_b = _dot((z * jax.nn.sigmoid(z)).astype(BF16), wco_ref[...])

    cq_ref[0] = _rms(_dot(h, wcq_ref[...]), qng_ref[...]).astype(BF16)
    ckv = _rms(_dot(h, wckv_ref[...]), kvng_ref[...])
    ckv_ref[0] = ckv.astype(BF16)
    for k in range(tm // CK):
        ckvt_ref[0, k, 0:KV_LORA, :] = ckv[k * CK:(k + 1) * CK, :].T.astype(BF16)
        ckvt_ref[0, k, KV_LORA:KV_LORA + BF16_ROWS, :] = jnp.ones((BF16_ROWS, CK), BF16)
    small = _dot(h, wsm_ref[...])
    kidx_ref[0] = _layer_norm(small[:, :D_IDX], ilg_ref[...], ilb_ref[...]).astype(BF16)
    small_t = small.T
    widxt_ref[0] = small_t[D_IDX:D_IDX + SUBLANES, :] * (N_IDX ** -0.5)

    qm = _dot(h, wqm_ref[...]).astype(BF16)
    heads = []
    for hh in range(N_HEADS_M):
        sl = slice(hh * D_HEAD_M, (hh + 1) * D_HEAD_M)
        s = _dot_nt(qm[:, sl], kmem_ref[:, sl]) * (D_HEAD_M ** -0.5)
        p = jnp.exp(s - jnp.max(s, axis=-1, keepdims=True))
        o = _dot(p.astype(BF16), vmem_ref[:, sl])
        heads.append((o / jnp.sum(p, axis=-1, keepdims=True)).astype(BF16))
    y_m = _dot(jnp.concatenate(heads, axis=-1), wmo_ref[...])

    mbm_ref[0] = (gbm_ref[:, :D_MODEL] * y_b + gbm_ref[:, D_MODEL:] * y_m).astype(BF16)


def _mix_pre(x, mem, p):
    b, t, d = x.shape
    tm = min(PRE_TM, t)
    assert t % tm == 0 and tm % CK == 0 and d == D_MODEL and mem.shape == (b, N_MEM, d)
    hm = N_HEADS_M * D_HEAD_M

    def tile(width):
        return pl.BlockSpec((1, tm, width), lambda i, j: (i, j, 0))

    consts = [p["mix_pre_g"], p["w_cq"], p["w_ckv"], p["w_small"], p["w_glu"], p["w_qmem"],
              p["w_gate"], p["q_norm_g"], p["kv_norm_g"], p["idx_ln_g"], p["idx_ln_b"], p["w_dw"],
              p["b_dw"], p["conv_ln_g"], p["conv_ln_b"], p["w_conv_out"], p["mem_norm_g"],
              p["w_mem_kv"], p["w_mem_o"]]
    kvt_rows = KV_LORA + BF16_ROWS
    out_shape = (
        jax.ShapeDtypeStruct((b, t, Q_LORA), BF16),
        jax.ShapeDtypeStruct((b, t, KV_LORA), BF16),
        jax.ShapeDtypeStruct((b, t // CK, kvt_rows, CK), BF16),
        jax.ShapeDtypeStruct((b, t, D_IDX), BF16),
        jax.ShapeDtypeStruct((b, SUBLANES, t), F32),
        jax.ShapeDtypeStruct((b, t, d), BF16),
        jax.ShapeDtypeStruct((b, t, d), BF16),
    )
    out_specs = (
        tile(Q_LORA), tile(KV_LORA),
        pl.BlockSpec((1, tm // CK, kvt_rows, CK), lambda i, j: (i, j, 0, 0)),
        tile(D_IDX),
        pl.BlockSpec((1, SUBLANES, tm), lambda i, j: (i, 0, j)),
        tile(d), tile(d),
    )
    return pl.pallas_call(
        _mix_pre_kernel,
        grid=(b, t // tm),
        in_specs=[tile(d), pl.BlockSpec((1, N_MEM, d), lambda i, j: (i, 0, 0))]
        + [_const_spec(c.shape) for c in consts],
        out_specs=out_specs,
        out_shape=out_shape,
        scratch_shapes=[pltpu.VMEM((HALO + tm, CONV_CH), F32),
                        pltpu.VMEM((SUBLANES - 1, HALO + tm - SUBLANES, CONV_CH), F32),
                        pltpu.VMEM((N_MEM, hm), BF16), pltpu.VMEM((N_MEM, hm), BF16),
                        pltpu.VMEM((tm, 2 * d), F32), pltpu.VMEM((tm, CONV_CH), F32),
                        pltpu.VMEM((CONV_WIDTH, SUBLANES, CONV_CH), F32)],
        compiler_params=pltpu.CompilerParams(
            dimension_semantics=("arbitrary", "arbitrary"), vmem_limit_bytes=VMEM_LIMIT),
        name="mix_pre",
    )(x, mem, *consts)


def _mix_dsa_kernel(topk, cq_ref, widxt_ref, ckv_ref, ckvt_ref, kidx_ref,
                    wiqt_ref, wuqt_ref, wuk_ref, wuvt_ref, ltri_ref,
                    o_ref, sc_ref, acc_ref, qlat_ref, s_ref, lg_ref):
    qi = pl.program_id(1)
    nch = (qi * TQ + TQ + CK - 1) // CK

    cq = cq_ref[0]
    qrow = qi * TQ + lax.broadcasted_iota(I32, (1, TQ), 1)
    krow = lax.broadcasted_iota(I32, (CK, TQ), 0)

    qidx_t = (_dot_nt(wiqt_ref[...], cq) * (D_IDX ** -0.5)).astype(BF16)
    widx_t = widxt_ref[0]

    q_t = _dot_nt(wuqt_ref[...], cq).astype(BF16)
    for hh in range(N_HEADS_A):
        qlat_ref[hh] = (_dot(wuk_ref[hh], q_t[hh * D_QK:(hh + 1) * D_QK, :])
                        * (D_QK ** -0.5 * LOG2E)).astype(BF16)
    acc_ref[...] = jnp.zeros_like(acc_ref)

    def fold(x, op):
        return op(x.reshape(CK // CNT_ROWS, CNT_ROWS, TQ), axis=0)

    def logits(c, hh):
        start = pl.multiple_of(c * CK, CK)
        lg_ref[hh] = _dot(kidx_ref[0, pl.ds(start, CK), :], qidx_t[hh * D_IDX:(hh + 1) * D_IDX, :])

    def combine(c, carry, next_chunk):
        n_ge0, n_gt0, s_max, s_min = carry
        start = pl.multiple_of(c * CK, CK)
        score = None
        for hh in range(N_IDX):
            term = widx_t[hh:hh + 1, :] * jnp.maximum(lg_ref[hh], 0.0)
            score = term if score is None else score + term
            if next_chunk is not None:
                logits(next_chunk, hh)
        causal = krow <= qrow - start
        score = jnp.where(causal, score, NEG_INF)
        sc_ref[pl.ds(start, CK), :] = score
        return (n_ge0 + fold(jnp.where(score >= 0.0, 1, 0), jnp.sum),
                n_gt0 + fold(jnp.where(score > 0.0, 1, 0), jnp.sum),
                jnp.maximum(s_max, fold(score, jnp.max)),
                jnp.minimum(s_min, fold(jnp.where(causal, score, jnp.inf), jnp.min)))

    zeros_i = jnp.zeros((CNT_ROWS, TQ), I32)
    for hh in range(N_IDX):
        logits(0, hh)
    stats = lax.fori_loop(1, nch, lambda c, carry: combine(c - 1, carry, c),
                          (zeros_i, zeros_i, jnp.full((CNT_ROWS, TQ), -jnp.inf, F32),
                           jnp.full((CNT_ROWS, TQ), jnp.inf, F32)))
    stats = combine(nch - 1, stats, None)
    n_ge0 = jnp.sum(stats[0], axis=0, keepdims=True)
    n_gt0 = jnp.sum(stats[1], axis=0, keepdims=True)
    s_max = jnp.max(stats[2], axis=0, keepdims=True)
    s_min = jnp.min(stats[3], axis=0, keepdims=True)
    n_causal = qrow + 1

    def count_ge(thr):
        def body(c, cnt):
            start = pl.multiple_of(c * CK, CK)
            return cnt + fold(jnp.where(sc_ref[pl.ds(start, CK), :] >= thr, 1, 0), jnp.sum)
        return jnp.sum(lax.fori_loop(0, nch, body, zeros_i), axis=0, keepdims=True)

    few = n_causal < topk
    zero_tie = jnp.where(n_gt0 < topk, jnp.where(n_ge0 >= topk, 1, 0), 0) == 1
    positive = n_gt0 >= topk
    lo = jnp.where(few, NEG_INF, jnp.where(positive, 0.0, jnp.where(zero_tie, 0.0, s_min)))
    hi = jnp.where(positive, jnp.minimum(s_max * (1.0 + 1e-6) + 1e-30, F32_BIG), 0.0)
    c_lo = jnp.where(few, topk, jnp.where(positive, n_ge0, jnp.where(zero_tie, n_ge0, n_causal)))
    c_hi = jnp.where(positive, 0, jnp.where(zero_tie, n_gt0, n_ge0))
    done = jnp.where(few, 1, jnp.where(zero_tie, 1, jnp.where(c_lo == topk, 1, 0)))

    def halve(state):
        lo, hi, c_lo, c_hi, done = state
        mid = lo + (hi - lo) * 0.5
        stuck = jnp.where(mid <= lo, 1, jnp.where(mid >= hi, 1, 0))
        c = count_ge(mid)
        live = jnp.where(done == 0, 1 - stuck, 0)
        raise_lo = jnp.where(c >= topk, live, 0) == 1
        lower_hi = jnp.where(c >= topk, 0, live) == 1
        lo, c_lo = jnp.where(raise_lo, mid, lo), jnp.where(raise_lo, c, c_lo)
        hi, c_hi = jnp.where(lower_hi, mid, hi), jnp.where(lower_hi, c, c_hi)
        return lo, hi, c_lo, c_hi, jnp.where(c_lo == topk, 1, jnp.maximum(done, stuck))

    unchecked = jnp.where(qi * TQ + TQ <= topk, 0, SEARCH_UNCHECKED)
    state = lax.fori_loop(0, unchecked, lambda _, st: halve(st), (lo, hi, c_lo, c_hi, done))

    def checked_halve(carry):
        state = halve(carry[2])
        return carry[0] + 1, jnp.min(state[4]), state

    _, _, (kth, _, n_ge, c_hi, _) = lax.while_loop(
        lambda carry: jnp.logical_and(carry[0] < SEARCH_CAP, carry[1] == 0), checked_halve,
        (jnp.int32(0), jnp.min(state[4]), state))
    n_tie_f = jnp.where(n_ge == topk, topk, topk - c_hi).astype(F32)

    def tie_prefix(c):
        start = pl.multiple_of(c * CK, CK)
        lg_ref[0] = _dot(ltri_ref[...], jnp.where(sc_ref[pl.ds(start, CK), :] == kth, 1.0, 0.0).astype(BF16))

    def mask_chunk(c, tie_seen):
        start = pl.multiple_of(c * CK, CK)
        sc = sc_ref[pl.ds(start, CK), :]
        tie_rank = lg_ref[0] + tie_seen
        tie_prefix(jnp.minimum(c + 1, nch - 1))
        bias = jnp.where(sc > kth, 0.0,
                         jnp.where(sc == kth, jnp.where(tie_rank <= n_tie_f, 0.0, NEG_INF), NEG_INF))
        sc_ref[pl.ds(start, CK), :] = jnp.where(krow <= qrow - start, bias, NEG_INF)
        return tie_rank[CK - 1:CK, :]


    def scores(c, hh):
        start = pl.multiple_of(c * CK, CK)
        s = _dot(ckv_ref[0, pl.ds(start, CK), :], qlat_ref[hh]) + sc_ref[pl.ds(start, CK), :]
        s_ref[hh] = s
        return jnp.max(s, axis=0, keepdims=True)

    def accumulate(c, hh, s_max, m_run, l_run):
        m_new = jnp.maximum(m_run, s_max)
        p = jnp.exp2(s_ref[hh] - jnp.maximum(m_new, 0.1 * NEG_INF)).astype(BF16)
        alpha = jnp.exp2(m_run - m_new)
        pv = _dot(ckvt_ref[0, c], p)
        acc_ref[hh] = alpha * acc_ref[hh] + pv[0:KV_LORA, :]
        return m_new, alpha * l_run + pv[KV_LORA:KV_LORA + 1, :]

    def attn_body(c, carry):
        m_run, l_run, s_max, tie_seen = carry
        tie_seen = mask_chunk(c, tie_seen)
        m_out, l_out, s_out = [], [], []
        for hh in range(N_HEADS_A):
            m_new, l_new = accumulate(c - 1, hh, s_max[hh], m_run[hh], l_run[hh])
            s_out.append(scores(c, hh))
            m_out.append(m_new)
            l_out.append(l_new)
        return tuple(m_out), tuple(l_out), tuple(s_out), tie_seen

    heads = range(N_HEADS_A)
    tie_prefix(0)
    tie_seen = mask_chunk(0, jnp.zeros((1, TQ), F32))
    init = (tuple(jnp.full((1, TQ), NEG_INF, F32) for _ in heads),
            tuple(jnp.zeros((1, TQ), F32) for _ in heads),
            tuple(scores(0, hh) for hh in heads), tie_seen)
    m_run, l_run, s_max, _ = lax.fori_loop(1, nch, attn_body, init)
    l_fin = [accumulate(nch - 1, hh, s_max[hh], m_run[hh], l_run[hh])[1] for hh in heads]

    o_t = jnp.concatenate(
        [_dot(wuvt_ref[hh], (acc_ref[hh] * (1.0 / l_fin[hh])).astype(BF16)) for hh in range(N_HEADS_A)],
        axis=0)
    o_ref[0] = o_t.T.astype(BF16)


def _mix_dsa(pre, p, t):
    cq, ckv, ckvt, kidx, widxt = pre
    b = cq.shape[0]
    assert t % CK == 0 and t % TQ == 0
    topk = min(TOPK_MAX, t // 4)
    assert topk <= CK
    ltri = (lax.broadcasted_iota(I32, (CK, CK), 1) <= lax.broadcasted_iota(I32, (CK, CK), 0)).astype(BF16)
    consts = [p["w_idx_q_t"], p["w_uq_t"], p["w_uk_h"], p["w_uv_t"], ltri]

    def whole(shape):
        nd = len(shape)
        return pl.BlockSpec((1,) + tuple(shape[1:]), lambda i, j: (i,) + (0,) * (nd - 1))

    return pl.pallas_call(
        functools.partial(_mix_dsa_kernel, topk),
        grid=(b, t // TQ),
        in_specs=[pl.BlockSpec((1, TQ, Q_LORA), lambda i, j: (i, j, 0)),
                  pl.BlockSpec((1, SUBLANES, TQ), lambda i, j: (i, 0, j)),
                  whole(ckv.shape), whole(ckvt.shape), whole(kidx.shape)]
        + [_const_spec(c.shape) for c in consts],
        out_specs=pl.BlockSpec((1, TQ, N_HEADS_A * D_V), lambda i, j: (i, j, 0)),
        out_shape=jax.ShapeDtypeStruct((b, t, N_HEADS_A * D_V), BF16),
        scratch_shapes=[pltpu.VMEM((t, TQ), F32), pltpu.VMEM((N_HEADS_A, KV_LORA, TQ), F32),
                        pltpu.VMEM((N_HEADS_A, KV_LORA, TQ), BF16), pltpu.VMEM((N_HEADS_A, CK, TQ), F32),
                        pltpu.VMEM((N_IDX, CK, TQ), F32)],
        compiler_params=pltpu.CompilerParams(
            dimension_semantics=("arbitrary", "arbitrary"), vmem_limit_bytes=VMEM_LIMIT),
        name="mix_dsa",
    )(cq, widxt, ckv, ckvt, kidx, *consts)


def _mixer_params(mix_pre_g, mix_post_g, w_in, q_norm_g, kv_norm_g, w_uq, w_uk, w_uv, w_idx_q,
                  idx_ln_g, idx_ln_b, w_dsa_o, w_dw, b_dw, conv_ln_g, conv_ln_b, w_conv_out,
                  mem_norm_g, w_mem_kv, w_mem_o, w_out):
    hm = N_HEADS_M * D_HEAD_M
    cuts = [0]
    for width in (Q_LORA, KV_LORA, D_IDX, N_IDX, 2 * CONV_CH, hm, 3 * D_MODEL):
        cuts.append(cuts[-1] + width)
    assert w_in.shape == (D_MODEL, cuts[-1])
    w_small = jnp.pad(w_in[:, cuts[2]:cuts[4]], ((0, 0), (0, LANES - D_IDX - N_IDX)))
    row = lambda v: v.reshape(1, -1).astype(F32)
    return {
        "mix_pre_g": row(mix_pre_g), "mix_post_g": row(mix_post_g),
        "w_cq": w_in[:, cuts[0]:cuts[1]].astype(BF16), "w_ckv": w_in[:, cuts[1]:cuts[2]].astype(BF16),
        "w_small": w_small.astype(BF16),
        "w_glu": w_in[:, cuts[4]:cuts[5]].astype(BF16), "w_qmem": w_in[:, cuts[5]:cuts[6]].astype(BF16),
        "w_gate": w_in[:, cuts[6]:cuts[7]].astype(BF16),
        "q_norm_g": row(q_norm_g), "kv_norm_g": row(kv_norm_g),
        "idx_ln_g": row(idx_ln_g), "idx_ln_b": row(idx_ln_b),
        "w_dw": w_dw.astype(F32), "b_dw": row(b_dw),
        "conv_ln_g": row(conv_ln_g), "conv_ln_b": row(conv_ln_b),
        "w_conv_out": w_conv_out.astype(BF16), "mem_norm_g": row(mem_norm_g),
        "w_mem_kv": w_mem_kv.astype(BF16), "w_mem_o": w_mem_o.astype(BF16),
        "w_idx_q_t": w_idx_q.T.astype(BF16), "w_uq_t": w_uq.T.astype(BF16),
        "w_uk_h": jnp.transpose(w_uk, (1, 0, 2)).astype(BF16),
        "w_uv_t": jnp.transpose(w_uv, (1, 2, 0)).astype(BF16),
        "w_dsa_o": w_dsa_o.astype(BF16), "w_out": w_out.astype(BF16),
    }


def kernel(x, mem, ffn1_pre_g, ffn1_post_g, ffn1_w_gu, ffn1_w_down, mix_pre_g, mix_post_g, w_in, q_norm_g, kv_norm_g, w_uq, w_uk, w_uv, w_idx_q, idx_ln_g, idx_ln_b, w_dsa_o, w_dw, b_dw, conv_ln_g, conv_ln_b, w_conv_out, mem_norm_g, w_mem_kv, w_mem_o, w_out, ffn2_pre_g, ffn2_post_g, ffn2_w_gu, ffn2_w_down):
    b, t, d = x.shape
    n = b * t
    for l in range(ffn1_pre_g.shape[0]):
        x = _ffn(x.reshape(n, d), ffn1_pre_g[l], ffn1_post_g[l], ffn1_w_gu[l], ffn1_w_down[l])
        p = _mixer_params(mix_pre_g[l], mix_post_g[l], w_in[l], q_norm_g[l], kv_norm_g[l], w_uq[l],
                          w_uk[l], w_uv[l], w_idx_q[l], idx_ln_g[l], idx_ln_b[l], w_dsa_o[l], w_dw[l],
                          b_dw[l], conv_ln_g[l], conv_ln_b[l], w_conv_out[l], mem_norm_g[l],
                          w_mem_kv[l], w_mem_o[l], w_out[l])
        cq, ckv, ckvt, kidx, widxt, ga, mbm = _mix_pre(x.reshape(b, t, d), mem, p)
        oa = _mix_dsa((cq, ckv, ckvt, kidx, widxt), p, t)
        x = _post_ffn(x, oa.reshape(n, -1), ga.reshape(n, d), mbm.reshape(n, d), p,
                      ffn2_pre_g[l], ffn2_post_g[l], ffn2_w_gu[l], ffn2_w_down[l])
        x = x.reshape(b, t, d)
    return x
```

```python
import functools
import math

import jax
import jax.numpy as jnp
from jax import lax
from jax.experimental import pallas as pl
from jax.experimental.pallas import tpu as pltpu

F32 = jnp.float32
BF16 = jnp.bfloat16
I32 = jnp.int32

EPS = 1e-6
NEG_INF = -1e30
D_MODEL = 1024
D_FF = 2816
N_HEADS_A = 8
D_QK = 64
D_V = 64
Q_LORA = 256
KV_LORA = 256
N_IDX = 4
D_IDX = 64
TOPK_MAX = 256
CONV_CH = 512
CONV_WIDTH = 31
N_HEADS_M = 4
D_HEAD_M = 128
N_MEM = 256

LANES = 128
SUBLANES = 8
BF16_ROWS = 16
VMEM_LIMIT = 56 * 1024 * 1024
F32_BIG = 3.0e38
SEARCH_CAP = 288
SEARCH_UNCHECKED = 19
LOG2E = math.log2(math.e)

FFN_TM = 512
FFN_CHUNKS = ((0, 1536), (1536, 1280))
PRE_TM = 512
HALO = 32
GATE_COLS = 256
CONV_ROWS = 16
TQ = 256
CK = 256
CNT_ROWS = 32


def _rms(x, g):
    return x * lax.rsqrt(jnp.mean(x * x, axis=-1, keepdims=True) + EPS) * g


def _layer_norm(x, g, b):
    mu = jnp.mean(x, axis=-1, keepdims=True)
    xc = x - mu
    var = jnp.mean(xc * xc, axis=-1, keepdims=True)
    return xc * lax.rsqrt(var + EPS) * g + b


def _dot(a, b):
    return jnp.dot(a, b, preferred_element_type=F32)


def _dot_nt(a, b):
    return lax.dot_general(a, b, (((1,), (1,)), ((), ())), preferred_element_type=F32)


def _const_spec(shape):
    nd = len(shape)
    return pl.BlockSpec(shape, lambda *_: (0,) * nd, pipeline_mode=pl.Buffered(1))


def _ffn_body(x, gpre_ref, gpost_ref, wg_ref, wu_ref, wd_ref):
    h = _rms(x, gpre_ref[...]).astype(BF16)
    acc = None
    for start, size in FFN_CHUNKS:
        g = _dot(h, wg_ref[:, start:start + size])
        u = _dot(h, wu_ref[:, start:start + size])
        a = (g * jax.nn.sigmoid(g) * u).astype(BF16)
        part = _dot(a, wd_ref[start:start + size, :])
        acc = part if acc is None else acc + part
    return x + 0.5 * _rms(acc, gpost_ref[...])


def _ffn_kernel(x_ref, gpre_ref, gpost_ref, wg_ref, wu_ref, wd_ref, o_ref):
    o_ref[...] = _ffn_body(x_ref[...], gpre_ref, gpost_ref, wg_ref, wu_ref, wd_ref)


def _post_ffn_kernel(x_ref, oa_ref, ga_ref, mbm_ref, wdo_ref, wout_ref, gmix_ref,
                     gpre_ref, gpost_ref, wg_ref, wu_ref, wd_ref, o_ref):
    y_a = _dot(oa_ref[...], wdo_ref[...])
    merged = ga_ref[...].astype(F32) * y_a + mbm_ref[...].astype(F32)
    x = x_ref[...] + _rms(_dot(merged.astype(BF16), wout_ref[...]), gmix_ref[...])
    o_ref[...] = _ffn_body(x, gpre_ref, gpost_ref, wg_ref, wu_ref, wd_ref)


def _ffn_weights(g_pre, g_post, w_gu, w_down):
    d = g_pre.shape[-1]
    return [g_pre.reshape(1, d), g_post.reshape(1, d), w_gu[:, :D_FF].astype(BF16),
            w_gu[:, D_FF:].astype(BF16), w_down.astype(BF16)]


def _ffn(x2d, g_pre, g_post, w_gu, w_down):
    n, d = x2d.shape
    assert n % FFN_TM == 0 and d == D_MODEL
    consts = _ffn_weights(g_pre, g_post, w_gu, w_down)
    row = pl.BlockSpec((FFN_TM, d), lambda i: (i, 0))
    return pl.pallas_call(
        _ffn_kernel,
        grid=(n // FFN_TM,),
        in_specs=[row] + [_const_spec(c.shape) for c in consts],
        out_specs=row,
        out_shape=jax.ShapeDtypeStruct((n, d), F32),
        compiler_params=pltpu.CompilerParams(
            dimension_semantics=("arbitrary",), vmem_limit_bytes=VMEM_LIMIT),
        name="ffn",
    )(x2d, *consts)


def _post_ffn(x2d, oa2d, ga2d, mbm2d, p, g_pre, g_post, w_gu, w_down):
    n, d = x2d.shape
    assert n % FFN_TM == 0 and d == D_MODEL
    consts = [p["w_dsa_o"], p["w_out"], p["mix_post_g"]] + _ffn_weights(g_pre, g_post, w_gu, w_down)

    def row(width):
        return pl.BlockSpec((FFN_TM, width), lambda i: (i, 0))

    return pl.pallas_call(
        _post_ffn_kernel,
        grid=(n // FFN_TM,),
        in_specs=[row(d), row(oa2d.shape[1]), row(d), row(d)] + [_const_spec(c.shape) for c in consts],
        out_specs=row(d),
        out_shape=jax.ShapeDtypeStruct((n, d), F32),
        compiler_params=pltpu.CompilerParams(
            dimension_semantics=("arbitrary",), vmem_limit_bytes=VMEM_LIMIT),
        name="post_ffn",
    )(x2d, oa2d, ga2d, mbm2d, *consts)


def _mix_pre_kernel(x_ref, mem_ref, gpre_ref, wcq_ref, wckv_ref, wsm_ref, wglu_ref, wqm_ref,
                    wgate_ref, qng_ref, kvng_ref, ilg_ref, ilb_ref, wdw_ref, bdw_ref, clg_ref,
                    clb_ref, wco_ref, mng_ref, wmkv_ref, wmo_ref,
                    cq_ref, ckv_ref, ckvt_ref, kidx_ref, widxt_ref, ga_ref, mbm_ref,
                    ubuf_ref, ushift_ref, kmem_ref, vmem_ref, gbm_ref, yconv_ref, wtap_ref):
    j = pl.program_id(1)
    tm = x_ref.shape[1]
    hm = N_HEADS_M * D_HEAD_M

    @pl.when(j == 0)
    def _():
        ubuf_ref[0:HALO, :] = jnp.zeros((HALO, CONV_CH), F32)
        for t in range(CONV_WIDTH):
            wtap_ref[t] = jnp.broadcast_to(wdw_ref[t:t + 1, :], (SUBLANES, CONV_CH))
        mk = _rms(mem_ref[0], mng_ref[...]).astype(BF16)
        kv = _dot(mk, wmkv_ref[...])
        kmem_ref[...] = kv[:, :hm].astype(BF16)
        vmem_ref[...] = kv[:, hm:].astype(BF16)

    h = _rms(x_ref[0], gpre_ref[...]).astype(BF16)

    uglu = _dot_nt(h, wglu_ref[...])
    ubuf_ref[HALO:HALO + tm, :] = uglu[:, :CONV_CH] * jax.nn.sigmoid(uglu[:, CONV_CH:])
    span = tm + HALO - SUBLANES
    for r in range(1, SUBLANES):
        ushift_ref[r - 1] = ubuf_ref[r:r + span, :]

    ties = []
    for c0 in range(0, 3 * D_MODEL, GATE_COLS):
        g = jax.nn.sigmoid(_dot_nt(h, wgate_ref[c0:c0 + GATE_COLS, :]))
        if c0 < D_MODEL:
            ga_ref[0, :, c0:c0 + GATE_COLS] = g.astype(BF16)
        else:
            gbm_ref[:, c0 - D_MODEL:c0 - D_MODEL + GATE_COLS] = g
        ties.append(g[0:SUBLANES, 0:LANES] * 0.0)

    first = HALO - (CONV_WIDTH - 1)
    n_blocks = tm // CONV_ROWS
    for blk in range(n_blocks):
        r0 = blk * CONV_ROWS
        y = None
        for t in range(CONV_WIDTH):
            shift, base = (first + t) % SUBLANES, (first + t) // SUBLANES * SUBLANES + r0
            rows = (ubuf_ref[base:base + CONV_ROWS, :] if shift == 0
                    else ushift_ref[shift - 1, base:base + CONV_ROWS, :])
            term = wtap_ref[t][None] * rows.reshape(CONV_ROWS // SUBLANES, SUBLANES, CONV_CH)
            y = term if y is None else y + term
        tie = jnp.tile(ties[blk * len(ties) // n_blocks], (CONV_ROWS // SUBLANES, CONV_CH // LANES))
        yconv_ref[r0:r0 + CONV_ROWS, :] = y.reshape(CONV_ROWS, CONV_CH) + bdw_ref[...] + tie
    ubuf_ref[0:HALO, :] = ubuf_ref[tm:tm + HALO, :]
    z = _layer_norm(yconv_ref[...], clg_ref[...], clb_ref[...])
    y_b = _dot((z * jax.nn.sigmoid(z)).astype(BF16), wco_ref[...])

    cq_ref[0] = _rms(_dot_nt(h, wcq_ref[...]), qng_ref[...]).astype(BF16)
    ckv = _rms(_dot_nt(h, wckv_ref[...]), kvng_ref[...])
    ckv_ref[0] = ckv.astype(BF16)
    for k in range(tm // CK):
        ckvt_ref[0, k, 0:KV_LORA, :] = ckv[k * CK:(k + 1) * CK, :].T.astype(BF16)
        ckvt_ref[0, k, KV_LORA:KV_LORA + BF16_ROWS, :] = jnp.ones((BF16_ROWS, CK), BF16)
    small = _dot_nt(h, wsm_ref[...])
    kidx_ref[0] = _layer_norm(small[:, :D_IDX], ilg_ref[...], ilb_ref[...]).astype(BF16)
    small_t = small.T
    widxt_ref[0] = small_t[D_IDX:D_IDX + SUBLANES, :] * (N_IDX ** -0.5)

    qm = _dot_nt(h, wqm_ref[...]).astype(BF16)
    heads = []
    for hh in range(N_HEADS_M):
        sl = slice(hh * D_HEAD_M, (hh + 1) * D_HEAD_M)
        s = _dot_nt(qm[:, sl], kmem_ref[:, sl]) * (D_HEAD_M ** -0.5)
        p = jnp.exp(s - jnp.max(s, axis=-1, keepdims=True))
        o = _dot(p.astype(BF16), vmem_ref[:, sl])
        heads.append((o / jnp.sum(p, axis=-1, keepdims=True)).astype(BF16))
    y_m = _dot(jnp.concatenate(heads, axis=-1), wmo_ref[...])

    mbm_ref[0] = (gbm_ref[:, :D_MODEL] * y_b + gbm_ref[:, D_MODEL:] * y_m).astype(BF16)


def _mix_pre(x, mem, p):
    b, t, d = x.shape
    tm = min(PRE_TM, t)
    assert t % tm == 0 and tm % CK == 0 and d == D_MODEL and mem.shape == (b, N_MEM, d)
    hm = N_HEADS_M * D_HEAD_M

    def tile(width):
        return pl.BlockSpec((1, tm, width), lambda i, j: (i, j, 0))

    consts = [p["mix_pre_g"], p["w_cq"], p["w_ckv"], p["w_small"], p["w_glu"], p["w_qmem"],
              p["w_gate"], p["q_norm_g"], p["kv_norm_g"], p["idx_ln_g"], p["idx_ln_b"], p["w_dw"],
              p["b_dw"], p["conv_ln_g"], p["conv_ln_b"], p["w_conv_out"], p["mem_norm_g"],
              p["w_mem_kv"], p["w_mem_o"]]
    kvt_rows = KV_LORA + BF16_ROWS
    out_shape = (
        jax.ShapeDtypeStruct((b, t, Q_LORA), BF16),
        jax.ShapeDtypeStruct((b, t, KV_LORA), BF16),
        jax.ShapeDtypeStruct((b, t // CK, kvt_rows, CK), BF16),
        jax.ShapeDtypeStruct((b, t, D_IDX), BF16),
        jax.ShapeDtypeStruct((b, SUBLANES, t), F32),
        jax.ShapeDtypeStruct((b, t, d), BF16),
        jax.ShapeDtypeStruct((b, t, d), BF16),
    )
    out_specs = (
        tile(Q_LORA), tile(KV_LORA),
        pl.BlockSpec((1, tm // CK, kvt_rows, CK), lambda i, j: (i, j, 0, 0)),
        tile(D_IDX),
        pl.BlockSpec((1, SUBLANES, tm), lambda i, j: (i, 0, j)),
        tile(d), tile(d),
    )
    return pl.pallas_call(
        _mix_pre_kernel,
        grid=(b, t // tm),
        in_specs=[tile(d), pl.BlockSpec((1, N_MEM, d), lambda i, j: (i, 0, 0))]
        + [_const_spec(c.shape) for c in consts],
        out_specs=out_specs,
        out_shape=out_shape,
        scratch_shapes=[pltpu.VMEM((HALO + tm, CONV_CH), F32),
                        pltpu.VMEM((SUBLANES - 1, HALO + tm - SUBLANES, CONV_CH), F32),
                        pltpu.VMEM((N_MEM, hm), BF16), pltpu.VMEM((N_MEM, hm), BF16),
                        pltpu.VMEM((tm, 2 * d), F32), pltpu.VMEM((tm, CONV_CH), F32),
                        pltpu.VMEM((CONV_WIDTH, SUBLANES, CONV_CH), F32)],
        compiler_params=pltpu.CompilerParams(
            dimension_semantics=("arbitrary", "arbitrary"), vmem_limit_bytes=VMEM_LIMIT),
        name="mix_pre",
    )(x, mem, *consts)


def _mix_dsa_kernel(topk, cq_ref, widxt_ref, ckv_ref, ckvt_ref, kidx_ref,
                    wiqt_ref, wuqt_ref, wuk_ref, wuvt_ref, ltri_ref,
                    o_ref, sc_ref, acc_ref, qlat_ref, s_ref, lg_ref):
    qi = pl.program_id(1)
    nch = (qi * TQ + TQ + CK - 1) // CK

    cq = cq_ref[0]
    qrow = qi * TQ + lax.broadcasted_iota(I32, (1, TQ), 1)
    krow = lax.broadcasted_iota(I32, (CK, TQ), 0)

    qidx_t = (_dot_nt(wiqt_ref[...], cq) * (D_IDX ** -0.5)).astype(BF16)
    widx_t = widxt_ref[0]

    q_t = _dot_nt(wuqt_ref[...], cq).astype(BF16)
    for hh in range(N_HEADS_A):
        qlat_ref[hh] = (_dot(wuk_ref[hh], q_t[hh * D_QK:(hh + 1) * D_QK, :])
                        * (D_QK ** -0.5 * LOG2E)).astype(BF16)
    acc_ref[...] = jnp.zeros_like(acc_ref)

    def fold(x, op):
        return op(x.reshape(CK // CNT_ROWS, CNT_ROWS, TQ), axis=0)

    def logits(c, hh):
        start = pl.multiple_of(c * CK, CK)
        lg_ref[hh] = _dot(kidx_ref[0, pl.ds(start, CK), :], qidx_t[hh * D_IDX:(hh + 1) * D_IDX, :])

    def combine(c, carry, next_chunk):
        n_ge0, n_gt0, s_max, s_min = carry
        start = pl.multiple_of(c * CK, CK)
        score = None
        for hh in range(N_IDX):
            term = widx_t[hh:hh + 1, :] * jnp.maximum(lg_ref[hh], 0.0)
            score = term if score is None else score + term
            if next_chunk is not None:
                logits(next_chunk, hh)
        causal = krow <= qrow - start
        score = jnp.where(causal, score, NEG_INF)
        sc_ref[pl.ds(start, CK), :] = score
        return (n_ge0 + fold(jnp.where(score >= 0.0, 1, 0), jnp.sum),
                n_gt0 + fold(jnp.where(score > 0.0, 1, 0), jnp.sum),
                jnp.maximum(s_max, fold(score, jnp.max)),
                jnp.minimum(s_min, fold(jnp.where(causal, score, jnp.inf), jnp.min)))

    zeros_i = jnp.zeros((CNT_ROWS, TQ), I32)
    for hh in range(N_IDX):
        logits(0, hh)
    stats = lax.fori_loop(1, nch, lambda c, carry: combine(c - 1, carry, c),
                          (zeros_i, zeros_i, jnp.full((CNT_ROWS, TQ), -jnp.inf, F32),
                           jnp.full((CNT_ROWS, TQ), jnp.inf, F32)))
    stats = combine(nch - 1, stats, None)
    n_ge0 = jnp.sum(stats[0], axis=0, keepdims=True)
    n_gt0 = jnp.sum(stats[1], axis=0, keepdims=True)
    s_max = jnp.max(stats[2], axis=0, keepdims=True)
    s_min = jnp.min(stats[3], axis=0, keepdims=True)
    n_causal = qrow + 1

    def count_ge(thr):
        def body(c, cnt):
            start = pl.multiple_of(c * CK, CK)
            return cnt + fold(jnp.where(sc_ref[pl.ds(start, CK), :] >= thr, 1, 0), jnp.sum)
        return jnp.sum(lax.fori_loop(0, nch, body, zeros_i), axis=0, keepdims=True)

    few = n_causal < topk
    zero_tie = jnp.where(n_gt0 < topk, jnp.where(n_ge0 >= topk, 1, 0), 0) == 1
    positive = n_gt0 >= topk
    lo = jnp.where(few, NEG_INF, jnp.where(positive, 0.0, jnp.where(zero_tie, 0.0, s_min)))
    hi = jnp.where(positive, jnp.minimum(s_max * (1.0 + 1e-6) + 1e-30, F32_BIG), 0.0)
    c_lo = jnp.where(few, topk, jnp.where(positive, n_ge0, jnp.where(zero_tie, n_ge0, n_causal)))
    c_hi = jnp.where(positive, 0, jnp.where(zero_tie, n_gt0, n_ge0))
    done = jnp.where(few, 1, jnp.where(zero_tie, 1, jnp.where(c_lo == topk, 1, 0)))

    def halve(state):
        lo, hi, c_lo, c_hi, done = state
        mid = lo + (hi - lo) * 0.5
        stuck = jnp.where(mid <= lo, 1, jnp.where(mid >= hi, 1, 0))
        c = count_ge(mid)
        live = jnp.where(done == 0, 1 - stuck, 0)
        raise_lo = jnp.where(c >= topk, live, 0) == 1
        lower_hi = jnp.where(c >= topk, 0, live) == 1
        lo, c_lo = jnp.where(raise_lo, mid, lo), jnp.where(raise_lo, c, c_lo)
        hi, c_hi = jnp.where(lower_hi, mid, hi), jnp.where(lower_hi, c, c_hi)
        return lo, hi, c_lo, c_hi, jnp.where(c_lo == topk, 1, jnp.maximum(done, stuck))

    unchecked = jnp.where(qi * TQ + TQ <= topk, 0, SEARCH_UNCHECKED)
    state = lax.fori_loop(0, unchecked, lambda _, st: halve(st), (lo, hi, c_lo, c_hi, done))

    def checked_halve(carry):
        state = halve(carry[2])
        return carry[0] + 1, jnp.min(state[4]), state

    _, _, (kth, _, n_ge, c_hi, _) = lax.while_loop(
        lambda carry: jnp.logical_and(carry[0] < SEARCH_CAP, carry[1] == 0), checked_halve,
        (jnp.int32(0), jnp.min(state[4]), state))
    n_tie_f = jnp.where(n_ge == topk, topk, topk - c_hi).astype(F32)

    def tie_prefix(c):
        start = pl.multiple_of(c * CK, CK)
        lg_ref[0] = _dot(ltri_ref[...], jnp.where(sc_ref[pl.ds(start, CK), :] == kth, 1.0, 0.0).astype(BF16))

    def mask_chunk(c, tie_seen):
        start = pl.multiple_of(c * CK, CK)
        sc = sc_ref[pl.ds(start, CK), :]
        tie_rank = lg_ref[0] + tie_seen
        tie_prefix(jnp.minimum(c + 1, nch - 1))
        bias = jnp.where(sc > kth, 0.0,
                         jnp.where(sc == kth, jnp.where(tie_rank <= n_tie_f, 0.0, NEG_INF), NEG_INF))
        sc_ref[pl.ds(start, CK), :] = jnp.where(krow <= qrow - start, bias, NEG_INF)
        return tie_rank[CK - 1:CK, :]


    def scores(c, hh):
        start = pl.multiple_of(c * CK, CK)
        s = _dot(ckv_ref[0, pl.ds(start, CK), :], qlat_ref[hh]) + sc_ref[pl.ds(start, CK), :]
        s_ref[hh] = s
        return jnp.max(s, axis=0, keepdims=True)

    def accumulate(c, hh, s_max, m_run, l_run):
        m_new = jnp.maximum(m_run, s_max)
        p = jnp.exp2(s_ref[hh] - jnp.maximum(m_new, 0.1 * NEG_INF)).astype(BF16)
        alpha = jnp.exp2(m_run - m_new)
        pv = _dot(ckvt_ref[0, c], p)
        acc_ref[hh] = alpha * acc_ref[hh] + pv[0:KV_LORA, :]
        return m_new, alpha * l_run + pv[KV_LORA:KV_LORA + 1, :]

    def attn_body(c, carry):
        m_run, l_run, s_max, tie_seen = carry
        tie_seen = mask_chunk(c, tie_seen)
        m_out, l_out, s_out = [], [], []
        for hh in range(N_HEADS_A):
            m_new, l_new = accumulate(c - 1, hh, s_max[hh], m_run[hh], l_run[hh])
            s_out.append(scores(c, hh))
            m_out.append(m_new)
            l_out.append(l_new)
        return tuple(m_out), tuple(l_out), tuple(s_out), tie_seen

    heads = range(N_HEADS_A)
    tie_prefix(0)
    tie_seen = mask_chunk(0, jnp.zeros((1, TQ), F32))
    init = (tuple(jnp.full((1, TQ), NEG_INF, F32) for _ in heads),
            tuple(jnp.zeros((1, TQ), F32) for _ in heads),
            tuple(scores(0, hh) for hh in heads), tie_seen)
    m_run, l_run, s_max, _ = lax.fori_loop(1, nch, attn_body, init)
    l_fin = [accumulate(nch - 1, hh, s_max[hh], m_run[hh], l_run[hh])[1] for hh in heads]

    o_t = jnp.concatenate(
        [_dot(wuvt_ref[hh], (acc_ref[hh] * (1.0 / l_fin[hh])).astype(BF16)) for hh in range(N_HEADS_A)],
        axis=0)
    o_ref[0] = o_t.T.astype(BF16)


def _mix_dsa(pre, p, t):
    cq, ckv, ckvt, kidx, widxt = pre
    b = cq.shape[0]
    assert t % CK == 0 and t % TQ == 0
    topk = min(TOPK_MAX, t // 4)
    assert topk <= CK
    ltri = (lax.broadcasted_iota(I32, (CK, CK), 1) <= lax.broadcasted_iota(I32, (CK, CK), 0)).astype(BF16)
    consts = [p["w_idx_q_t"], p["w_uq_t"], p["w_uk_h"], p["w_uv_t"], ltri]

    def whole(shape):
        nd = len(shape)
        return pl.BlockSpec((1,) + tuple(shape[1:]), lambda i, j: (i,) + (0,) * (nd - 1))

    return pl.pallas_call(
        functools.partial(_mix_dsa_kernel, topk),
        grid=(b, t // TQ),
        in_specs=[pl.BlockSpec((1, TQ, Q_LORA), lambda i, j: (i, j, 0)),
                  pl.BlockSpec((1, SUBLANES, TQ), lambda i, j: (i, 0, j)),
                  whole(ckv.shape), whole(ckvt.shape), whole(kidx.shape)]
        + [_const_spec(c.shape) for c in consts],
        out_specs=pl.BlockSpec((1, TQ, N_HEADS_A * D_V), lambda i, j: (i, j, 0)),
        out_shape=jax.ShapeDtypeStruct((b, t, N_HEADS_A * D_V), BF16),
        scratch_shapes=[pltpu.VMEM((t, TQ), F32), pltpu.VMEM((N_HEADS_A, KV_LORA, TQ), F32),
                        pltpu.VMEM((N_HEADS_A, KV_LORA, TQ), BF16), pltpu.VMEM((N_HEADS_A, CK, TQ), F32),
                        pltpu.VMEM((N_IDX, CK, TQ), F32)],
        compiler_params=pltpu.CompilerParams(
            dimension_semantics=("arbitrary", "arbitrary"), vmem_limit_bytes=VMEM_LIMIT),
        name="mix_dsa",
    )(cq, widxt, ckv, ckvt, kidx, *consts)


def _mixer_params(mix_pre_g, mix_post_g, w_in, q_norm_g, kv_norm_g, w_uq, w_uk, w_uv, w_idx_q,
                  idx_ln_g, idx_ln_b, w_dsa_o, w_dw, b_dw, conv_ln_g, conv_ln_b, w_conv_out,
                  mem_norm_g, w_mem_kv, w_mem_o, w_out):
    hm = N_HEADS_M * D_HEAD_M
    cuts = [0]
    for width in (Q_LORA, KV_LORA, D_IDX, N_IDX, 2 * CONV_CH, hm, 3 * D_MODEL):
        cuts.append(cuts[-1] + width)
    assert w_in.shape == (D_MODEL, cuts[-1])
    w_t = w_in.T
    w_small = jnp.pad(w_t[cuts[2]:cuts[4], :], ((0, LANES - D_IDX - N_IDX), (0, 0)))
    row = lambda v: v.reshape(1, -1).astype(F32)
    return {
        "mix_pre_g": row(mix_pre_g), "mix_post_g": row(mix_post_g),
        "w_cq": w_t[cuts[0]:cuts[1], :].astype(BF16), "w_ckv": w_t[cuts[1]:cuts[2], :].astype(BF16),
        "w_small": w_small.astype(BF16),
        "w_glu": w_t[cuts[4]:cuts[5], :].astype(BF16), "w_qmem": w_t[cuts[5]:cuts[6], :].astype(BF16),
        "w_gate": w_t[cuts[6]:cuts[7], :].astype(BF16),
        "q_norm_g": row(q_norm_g), "kv_norm_g": row(kv_norm_g),
        "idx_ln_g": row(idx_ln_g), "idx_ln_b": row(idx_ln_b),
        "w_dw": w_dw.astype(F32), "b_dw": row(b_dw),
        "conv_ln_g": row(conv_ln_g), "conv_ln_b": row(conv_ln_b),
        "w_conv_out": w_conv_out.astype(BF16), "mem_norm_g": row(mem_norm_g),
        "w_mem_kv": w_mem_kv.astype(BF16), "w_mem_o": w_mem_o.astype(BF16),
        "w_idx_q_t": w_idx_q.T.astype(BF16), "w_uq_t": w_uq.T.astype(BF16),
        "w_uk_h": jnp.transpose(w_uk, (1, 0, 2)).astype(BF16),
        "w_uv_t": jnp.transpose(w_uv, (1, 2, 0)).astype(BF16),
        "w_dsa_o": w_dsa_o.astype(BF16), "w_out": w_out.astype(BF16),
    }


def kernel(x, mem, ffn1_pre_g, ffn1_post_g, ffn1_w_gu, ffn1_w_down, mix_pre_g, mix_post_g, w_in, q_norm_g, kv_norm_g, w_uq, w_uk, w_uv, w_idx_q, idx_ln_g, idx_ln_b, w_dsa_o, w_dw, b_dw, conv_ln_g, conv_ln_b, w_conv_out, mem_norm_g, w_mem_kv, w_mem_o, w_out, ffn2_pre_g, ffn2_post_g, ffn2_w_gu, ffn2_w_down):
    b, t, d = x.shape
    n = b * t
    for l in range(ffn1_pre_g.shape[0]):
        x = _ffn(x.reshape(n, d), ffn1_pre_g[l], ffn1_post_g[l], ffn1_w_gu[l], ffn1_w_down[l])
        p = _mixer_params(mix_pre_g[l], mix_post_g[l], w_in[l], q_norm_g[l], kv_norm_g[l], w_uq[l],
                          w_uk[l], w_uv[l], w_idx_q[l], idx_ln_g[l], idx_ln_b[l], w_dsa_o[l], w_dw[l],
                          b_dw[l], conv_ln_g[l], conv_ln_b[l], w_conv_out[l], mem_norm_g[l],
                          w_mem_kv[l], w_mem_o[l], w_out[l])
        cq, ckv, ckvt, kidx, widxt, ga, mbm = _mix_pre(x.reshape(b, t, d), mem, p)
        oa = _mix_dsa((cq, ckv, ckvt, kidx, widxt), p, t)
        x = _post_ffn(x, oa.reshape(n, -1), ga.reshape(n, d), mbm.reshape(n, d), p,
                      ffn2_pre_g[l], ffn2_post_g[l], ffn2_w_gu[l], ffn2_w_down[l])
        x = x.reshape(b, t, d)
    return x
```

```python
import functools
import math

import jax
import jax.numpy as jnp
from jax import lax
from jax.experimental import pallas as pl
from jax.experimental.pallas import tpu as pltpu

F32 = jnp.float32
BF16 = jnp.bfloat16
I32 = jnp.int32

EPS = 1e-6
NEG_INF = -1e30
D_MODEL = 1024
D_FF = 2816
N_HEADS_A = 8
D_QK = 64
D_V = 64
Q_LORA = 256
KV_LORA = 256
N_IDX = 4
D_IDX = 64
TOPK_MAX = 256
CONV_CH = 512
CONV_WIDTH = 31
N_HEADS_M = 4
D_HEAD_M = 128
N_MEM = 256

LANES = 128
SUBLANES = 8
BF16_ROWS = 16
VMEM_LIMIT = 56 * 1024 * 1024
F32_BIG = 3.0e38
SEARCH_CAP = 288
SEARCH_UNCHECKED = 19
LOG2E = math.log2(math.e)

FFN_TM = 512
FFN_CHUNKS = ((0, 1536), (1536, 1280))
PRE_TM = 512
HALO = 32
GATE_COLS = 256
CONV_ROWS = 16
TQ = 256
CK = 256
CNT_ROWS = 32


def _rms(x, g):
    return x * lax.rsqrt(jnp.mean(x * x, axis=-1, keepdims=True) + EPS) * g


def _layer_norm(x, g, b):
    mu = jnp.mean(x, axis=-1, keepdims=True)
    xc = x - mu
    var = jnp.mean(xc * xc, axis=-1, keepdims=True)
    return xc * lax.rsqrt(var + EPS) * g + b


def _dot(a, b):
    return jnp.dot(a, b, preferred_element_type=F32)


def _dot_nt(a, b):
    return lax.dot_general(a, b, (((1,), (1,)), ((), ())), preferred_element_type=F32)


def _const_spec(shape):
    nd = len(shape)
    return pl.BlockSpec(shape, lambda *_: (0,) * nd, pipeline_mode=pl.Buffered(1))


def _ffn_body(x, gpre_ref, gpost_ref, wgu_ref, wd_ref):
    h = _rms(x, gpre_ref[...]).astype(BF16)
    acc = None
    for start, size in FFN_CHUNKS:
        g = _dot(h, wgu_ref[:, start:start + size])
        u = _dot(h, wgu_ref[:, D_FF + start:D_FF + start + size])
        a = (g * jax.nn.sigmoid(g) * u).astype(BF16)
        part = _dot(a, wd_ref[start:start + size, :])
        acc = part if acc is None else acc + part
    return x + 0.5 * _rms(acc, gpost_ref[...])


def _ffn_kernel(x_ref, gpre_ref, gpost_ref, wgu_ref, wd_ref, o_ref):
    o_ref[...] = _ffn_body(x_ref[...], gpre_ref, gpost_ref, wgu_ref, wd_ref)


def _post_ffn_kernel(x_ref, oa_ref, ga_ref, mbm_ref, wdo_ref, wout_ref, gmix_ref,
                     gpre_ref, gpost_ref, wgu_ref, wd_ref, o_ref):
    y_a = _dot(oa_ref[...], wdo_ref[...])
    merged = ga_ref[...].astype(F32) * y_a + mbm_ref[...].astype(F32)
    x = x_ref[...] + _rms(_dot(merged.astype(BF16), wout_ref[...]), gmix_ref[...])
    o_ref[...] = _ffn_body(x, gpre_ref, gpost_ref, wgu_ref, wd_ref)


def _ffn_weights(g_pre, g_post, w_gu, w_down):
    d = g_pre.shape[-1]
    assert w_gu.shape == (d, 2 * D_FF) and D_FF % LANES == 0
    return [g_pre.reshape(1, d), g_post.reshape(1, d), w_gu.astype(BF16), w_down.astype(BF16)]


def _ffn(x2d, g_pre, g_post, w_gu, w_down):
    n, d = x2d.shape
    assert n % FFN_TM == 0 and d == D_MODEL
    consts = _ffn_weights(g_pre, g_post, w_gu, w_down)
    row = pl.BlockSpec((FFN_TM, d), lambda i: (i, 0))
    return pl.pallas_call(
        _ffn_kernel,
        grid=(n // FFN_TM,),
        in_specs=[row] + [_const_spec(c.shape) for c in consts],
        out_specs=row,
        out_shape=jax.ShapeDtypeStruct((n, d), F32),
        compiler_params=pltpu.CompilerParams(
            dimension_semantics=("arbitrary",), vmem_limit_bytes=VMEM_LIMIT),
        name="ffn",
    )(x2d, *consts)


def _post_ffn(x2d, oa2d, ga2d, mbm2d, p, g_pre, g_post, w_gu, w_down):
    n, d = x2d.shape
    assert n % FFN_TM == 0 and d == D_MODEL
    consts = [p["w_dsa_o"], p["w_out"], p["mix_post_g"]] + _ffn_weights(g_pre, g_post, w_gu, w_down)

    def row(width):
        return pl.BlockSpec((FFN_TM, width), lambda i: (i, 0))

    return pl.pallas_call(
        _post_ffn_kernel,
        grid=(n // FFN_TM,),
        in_specs=[row(d), row(oa2d.shape[1]), row(d), row(d)] + [_const_spec(c.shape) for c in consts],
        out_specs=row(d),
        out_shape=jax.ShapeDtypeStruct((n, d), F32),
        compiler_params=pltpu.CompilerParams(
            dimension_semantics=("arbitrary",), vmem_limit_bytes=VMEM_LIMIT),
        name="post_ffn",
    )(x2d, oa2d, ga2d, mbm2d, *consts)


def _mix_pre_kernel(x_ref, mem_ref, gpre_ref, wcq_ref, wckv_ref, wsm_ref, wglu_ref, wqm_ref,
                    wgate_ref, qng_ref, kvng_ref, ilg_ref, ilb_ref, wdw_ref, bdw_ref, clg_ref,
                    clb_ref, wco_ref, mng_ref, wmkv_ref, wmo_ref,
                    cq_ref, ckv_ref, ckvt_ref, kidx_ref, widxt_ref, ga_ref, mbm_ref,
                    ubuf_ref, ushift_ref, kmem_ref, vmem_ref, gbm_ref, yconv_ref, wtap_ref):
    j = pl.program_id(1)
    tm = x_ref.shape[1]
    hm = N_HEADS_M * D_HEAD_M

    @pl.when(j == 0)
    def _():
        ubuf_ref[0:HALO, :] = jnp.zeros((HALO, CONV_CH), F32)
        for t in range(CONV_WIDTH):
            wtap_ref[t] = jnp.broadcast_to(wdw_ref[t:t + 1, :], (SUBLANES, CONV_CH))
        mk = _rms(mem_ref[0], mng_ref[...]).astype(BF16)
        kv = _dot(mk, wmkv_ref[...])
        kmem_ref[...] = kv[:, :hm].astype(BF16)
        vmem_ref[...] = kv[:, hm:].astype(BF16)

    h = _rms(x_ref[0], gpre_ref[...]).astype(BF16)

    uglu = _dot_nt(h, wglu_ref[...])
    ubuf_ref[HALO:HALO + tm, :] = uglu[:, :CONV_CH] * jax.nn.sigmoid(uglu[:, CONV_CH:])
    span = tm + HALO - SUBLANES
    for r in range(1, SUBLANES):
        ushift_ref[r - 1] = ubuf_ref[r:r + span, :]

    ties = []
    for c0 in range(0, 3 * D_MODEL, GATE_COLS):
        g = jax.nn.sigmoid(_dot_nt(h, wgate_ref[c0:c0 + GATE_COLS, :]))
        if c0 < D_MODEL:
            ga_ref[0, :, c0:c0 + GATE_COLS] = g.astype(BF16)
        else:
            gbm_ref[:, c0 - D_MODEL:c0 - D_MODEL + GATE_COLS] = g
        ties.append(g[0:SUBLANES, 0:LANES] * 0.0)

    first = HALO - (CONV_WIDTH - 1)
    n_blocks = tm // CONV_ROWS
    for blk in range(n_blocks):
        r0 = blk * CONV_ROWS
        y = None
        for t in range(CONV_WIDTH):
            shift, base = (first + t) % SUBLANES, (first + t) // SUBLANES * SUBLANES + r0
            rows = (ubuf_ref[base:base + CONV_ROWS, :] if shift == 0
                    else ushift_ref[shift - 1, base:base + CONV_ROWS, :])
            term = wtap_ref[t][None] * rows.reshape(CONV_ROWS // SUBLANES, SUBLANES, CONV_CH)
            y = term if y is None else y + term
        tie = jnp.tile(ties[blk * len(ties) // n_blocks], (CONV_ROWS // SUBLANES, CONV_CH // LANES))
        yconv_ref[r0:r0 + CONV_ROWS, :] = y.reshape(CONV_ROWS, CONV_CH) + bdw_ref[...] + tie
    ubuf_ref[0:HALO, :] = ubuf_ref[tm:tm + HALO, :]
    z = _layer_norm(yconv_ref[...], clg_ref[...], clb_ref[...])
    y_b = _dot((z * jax.nn.sigmoid(z)).astype(BF16), wco_ref[...])

    cq_ref[0] = _rms(_dot_nt(h, wcq_ref[...]), qng_ref[...]).astype(BF16)
    ckv = _rms(_dot_nt(h, wckv_ref[...]), kvng_ref[...])
    ckv_ref[0] = ckv.astype(BF16)
    for k in range(tm // CK):
        ckvt_ref[0, k, 0:KV_LORA, :] = ckv[k * CK:(k + 1) * CK, :].T.astype(BF16)
        ckvt_ref[0, k, KV_LORA:KV_LORA + BF16_ROWS, :] = jnp.ones((BF16_ROWS, CK), BF16)
    small = _dot_nt(h, wsm_ref[...])
    kidx_ref[0] = _layer_norm(small[:, :D_IDX], ilg_ref[...], ilb_ref[...]).astype(BF16)
    small_t = small.T
    widxt_ref[0] = small_t[D_IDX:D_IDX + SUBLANES, :] * (N_IDX ** -0.5)

    qm = _dot_nt(h, wqm_ref[...]).astype(BF16)
    heads = []
    for hh in range(N_HEADS_M):
        sl = slice(hh * D_HEAD_M, (hh + 1) * D_HEAD_M)
        s = _dot_nt(qm[:, sl], kmem_ref[:, sl]) * (D_HEAD_M ** -0.5)
        p = jnp.exp(s - jnp.max(s, axis=-1, keepdims=True))
        o = _dot(p.astype(BF16), vmem_ref[:, sl])
        heads.append((o / jnp.sum(p, axis=-1, keepdims=True)).astype(BF16))
    y_m = _dot(jnp.concatenate(heads, axis=-1), wmo_ref[...])

    mbm_ref[0] = (gbm_ref[:, :D_MODEL] * y_b + gbm_ref[:, D_MODEL:] * y_m).astype(BF16)


def _mix_pre(x, mem, p):
    b, t, d = x.shape
    tm = min(PRE_TM, t)
    assert t % tm == 0 and tm % CK == 0 and d == D_MODEL and mem.shape == (b, N_MEM, d)
    hm = N_HEADS_M * D_HEAD_M

    def tile(width):
        return pl.BlockSpec((1, tm, width), lambda i, j: (i, j, 0))

    consts = [p["mix_pre_g"], p["w_cq"], p["w_ckv"], p["w_small"], p["w_glu"], p["w_qmem"],
              p["w_gate"], p["q_norm_g"], p["kv_norm_g"], p["idx_ln_g"], p["idx_ln_b"], p["w_dw"],
              p["b_dw"], p["conv_ln_g"], p["conv_ln_b"], p["w_conv_out"], p["mem_norm_g"],
              p["w_mem_kv"], p["w_mem_o"]]
    kvt_rows = KV_LORA + BF16_ROWS
    out_shape = (
        jax.ShapeDtypeStruct((b, t, Q_LORA), BF16),
        jax.ShapeDtypeStruct((b, t, KV_LORA), BF16),
        jax.ShapeDtypeStruct((b, t // CK, kvt_rows, CK), BF16),
        jax.ShapeDtypeStruct((b, t, D_IDX), BF16),
        jax.ShapeDtypeStruct((b, SUBLANES, t), F32),
        jax.ShapeDtypeStruct((b, t, d), BF16),
        jax.ShapeDtypeStruct((b, t, d), BF16),
    )
    out_specs = (
        tile(Q_LORA), tile(KV_LORA),
        pl.BlockSpec((1, tm // CK, kvt_rows, CK), lambda i, j: (i, j, 0, 0)),
        tile(D_IDX),
        pl.BlockSpec((1, SUBLANES, tm), lambda i, j: (i, 0, j)),
        tile(d), tile(d),
    )
    return pl.pallas_call(
        _mix_pre_kernel,
        grid=(b, t // tm),
        in_specs=[tile(d), pl.BlockSpec((1, N_MEM, d), lambda i, j: (i, 0, 0))]
        + [_const_spec(c.shape) for c in consts],
        out_specs=out_specs,
        out_shape=out_shape,
        scratch_shapes=[pltpu.VMEM((HALO + tm, CONV_CH), F32),
                        pltpu.VMEM((SUBLANES - 1, HALO + tm - SUBLANES, CONV_CH), F32),
                        pltpu.VMEM((N_MEM, hm), BF16), pltpu.VMEM((N_MEM, hm), BF16),
                        pltpu.VMEM((tm, 2 * d), F32), pltpu.VMEM((tm, CONV_CH), F32),
                        pltpu.VMEM((CONV_WIDTH, SUBLANES, CONV_CH), F32)],
        compiler_params=pltpu.CompilerParams(
            dimension_semantics=("arbitrary", "arbitrary"), vmem_limit_bytes=VMEM_LIMIT),
        name="mix_pre",
    )(x, mem, *consts)


def _mix_dsa_kernel(topk, cq_ref, widxt_ref, ckv_ref, ckvt_ref, kidx_ref,
                    wiqt_ref, wuqt_ref, wuk_ref, wuvt_ref, ltri_ref,
                    o_ref, sc_ref, acc_ref, qlat_ref, s_ref, lg_ref):
    qi = pl.program_id(1)
    nch = (qi * TQ + TQ + CK - 1) // CK

    cq = cq_ref[0]
    qrow = qi * TQ + lax.broadcasted_iota(I32, (1, TQ), 1)
    krow = lax.broadcasted_iota(I32, (CK, TQ), 0)

    qidx_t = (_dot_nt(wiqt_ref[...], cq) * (D_IDX ** -0.5)).astype(BF16)
    widx_t = widxt_ref[0]

    q_t = _dot_nt(wuqt_ref[...], cq).astype(BF16)
    for hh in range(N_HEADS_A):
        qlat_ref[hh] = (_dot(wuk_ref[hh], q_t[hh * D_QK:(hh + 1) * D_QK, :])
                        * (D_QK ** -0.5 * LOG2E)).astype(BF16)
    acc_ref[...] = jnp.zeros_like(acc_ref)

    def fold(x, op):
        return op(x.reshape(CK // CNT_ROWS, CNT_ROWS, TQ), axis=0)

    def logits(c, hh):
        start = pl.multiple_of(c * CK, CK)
        lg_ref[hh] = _dot(kidx_ref[0, pl.ds(start, CK), :], qidx_t[hh * D_IDX:(hh + 1) * D_IDX, :])

    def combine(c, carry, next_chunk):
        n_ge0, n_gt0, s_max, s_min = carry
        start = pl.multiple_of(c * CK, CK)
        score = None
        for hh in range(N_IDX):
            term = widx_t[hh:hh + 1, :] * jnp.maximum(lg_ref[hh], 0.0)
            score = term if score is None else score + term
            if next_chunk is not None:
                logits(next_chunk, hh)
        causal = krow <= qrow - start
        score = jnp.where(causal, score, NEG_INF)
        sc_ref[pl.ds(start, CK), :] = score
        return (n_ge0 + fold(jnp.where(score >= 0.0, 1, 0), jnp.sum),
                n_gt0 + fold(jnp.where(score > 0.0, 1, 0), jnp.sum),
                jnp.maximum(s_max, fold(score, jnp.max)),
                jnp.minimum(s_min, fold(jnp.where(causal, score, jnp.inf), jnp.min)))

    zeros_i = jnp.zeros((CNT_ROWS, TQ), I32)
    for hh in range(N_IDX):
        logits(0, hh)
    stats = lax.fori_loop(1, nch, lambda c, carry: combine(c - 1, carry, c),
                          (zeros_i, zeros_i, jnp.full((CNT_ROWS, TQ), -jnp.inf, F32),
                           jnp.full((CNT_ROWS, TQ), jnp.inf, F32)))
    stats = combine(nch - 1, stats, None)
    n_ge0 = jnp.sum(stats[0], axis=0, keepdims=True)
    n_gt0 = jnp.sum(stats[1], axis=0, keepdims=True)
    s_max = jnp.max(stats[2], axis=0, keepdims=True)
    s_min = jnp.min(stats[3], axis=0, keepdims=True)
    n_causal = qrow + 1

    def count_ge(thr):
        def body(c, cnt):
            start = pl.multiple_of(c * CK, CK)
            return cnt + fold(jnp.where(sc_ref[pl.ds(start, CK), :] >= thr, 1, 0), jnp.sum)
        return jnp.sum(lax.fori_loop(0, nch, body, zeros_i), axis=0, keepdims=True)

    few = n_causal < topk
    zero_tie = jnp.where(n_gt0 < topk, jnp.where(n_ge0 >= topk, 1, 0), 0) == 1
    positive = n_gt0 >= topk
    lo = jnp.where(few, NEG_INF, jnp.where(positive, 0.0, jnp.where(zero_tie, 0.0, s_min)))
    hi = jnp.where(positive, jnp.minimum(s_max * (1.0 + 1e-6) + 1e-30, F32_BIG), 0.0)
    c_lo = jnp.where(few, topk, jnp.where(positive, n_ge0, jnp.where(zero_tie, n_ge0, n_causal)))
    c_hi = jnp.where(positive, 0, jnp.where(zero_tie, n_gt0, n_ge0))
    done = jnp.where(few, 1, jnp.where(zero_tie, 1, jnp.where(c_lo == topk, 1, 0)))

    def halve(state):
        lo, hi, c_lo, c_hi, done = state
        mid = lo + (hi - lo) * 0.5
        stuck = jnp.where(mid <= lo, 1, jnp.where(mid >= hi, 1, 0))
        c = count_ge(mid)
        live = jnp.where(done == 0, 1 - stuck, 0)
        raise_lo = jnp.where(c >= topk, live, 0) == 1
        lower_hi = jnp.where(c >= topk, 0, live) == 1
        lo, c_lo = jnp.where(raise_lo, mid, lo), jnp.where(raise_lo, c, c_lo)
        hi, c_hi = jnp.where(lower_hi, mid, hi), jnp.where(lower_hi, c, c_hi)
        return lo, hi, c_lo, c_hi, jnp.where(c_lo == topk, 1, jnp.maximum(done, stuck))

    unchecked = jnp.where(qi * TQ + TQ <= topk, 0, SEARCH_UNCHECKED)
    state = lax.fori_loop(0, unchecked, lambda _, st: halve(st), (lo, hi, c_lo, c_hi, done))

    def checked_halve(carry):
        state = halve(carry[2])
        return carry[0] + 1, jnp.min(state[4]), state

    _, _, (kth, _, n_ge, c_hi, _) = lax.while_loop(
        lambda carry: jnp.logical_and(carry[0] < SEARCH_CAP, carry[1] == 0), checked_halve,
        (jnp.int32(0), jnp.min(state[4]), state))
    n_tie_f = jnp.where(n_ge == topk, topk, topk - c_hi).astype(F32)

    def tie_prefix(c):
        start = pl.multiple_of(c * CK, CK)
        lg_ref[0] = _dot(ltri_ref[...], jnp.where(sc_ref[pl.ds(start, CK), :] == kth, 1.0, 0.0).astype(BF16))

    def mask_chunk(c, tie_seen):
        start = pl.multiple_of(c * CK, CK)
        sc = sc_ref[pl.ds(start, CK), :]
        tie_rank = lg_ref[0] + tie_seen
        tie_prefix(jnp.minimum(c + 1, nch - 1))
        bias = jnp.where(sc > kth, 0.0,
                         jnp.where(sc == kth, jnp.where(tie_rank <= n_tie_f, 0.0, NEG_INF), NEG_INF))
        sc_ref[pl.ds(start, CK), :] = jnp.where(krow <= qrow - start, bias, NEG_INF)
        return tie_rank[CK - 1:CK, :]


    def scores(c, hh):
        start = pl.multiple_of(c * CK, CK)
        s = _dot(ckv_ref[0, pl.ds(start, CK), :], qlat_ref[hh]) + sc_ref[pl.ds(start, CK), :]
        s_ref[hh] = s
        return jnp.max(s, axis=0, keepdims=True)

    def accumulate(c, hh, s_max, m_run, l_run):
        m_new = jnp.maximum(m_run, s_max)
        p = jnp.exp2(s_ref[hh] - jnp.maximum(m_new, 0.1 * NEG_INF)).astype(BF16)
        alpha = jnp.exp2(m_run - m_new)
        pv = _dot(ckvt_ref[0, c], p)
        acc_ref[hh] = alpha * acc_ref[hh] + pv[0:KV_LORA, :]
        return m_new, alpha * l_run + pv[KV_LORA:KV_LORA + 1, :]

    def attn_body(c, carry):
        m_run, l_run, s_max, tie_seen = carry
        tie_seen = mask_chunk(c, tie_seen)
        m_out, l_out, s_out = [], [], []
        for hh in range(N_HEADS_A):
            m_new, l_new = accumulate(c - 1, hh, s_max[hh], m_run[hh], l_run[hh])
            s_out.append(scores(c, hh))
            m_out.append(m_new)
            l_out.append(l_new)
        return tuple(m_out), tuple(l_out), tuple(s_out), tie_seen

    heads = range(N_HEADS_A)
    tie_prefix(0)
    tie_seen = mask_chunk(0, jnp.zeros((1, TQ), F32))
    init = (tuple(jnp.full((1, TQ), NEG_INF, F32) for _ in heads),
            tuple(jnp.zeros((1, TQ), F32) for _ in heads),
            tuple(scores(0, hh) for hh in heads), tie_seen)
    m_run, l_run, s_max, _ = lax.fori_loop(1, nch, attn_body, init)
    l_fin = [accumulate(nch - 1, hh, s_max[hh], m_run[hh], l_run[hh])[1] for hh in heads]

    o_t = jnp.concatenate(
        [_dot(wuvt_ref[hh], (acc_ref[hh] * (1.0 / l_fin[hh])).astype(BF16)) for hh in range(N_HEADS_A)],
        axis=0)
    o_ref[0] = o_t.T.astype(BF16)


def _mix_dsa(pre, p, t):
    cq, ckv, ckvt, kidx, widxt = pre
    b = cq.shape[0]
    assert t % CK == 0 and t % TQ == 0
    topk = min(TOPK_MAX, t // 4)
    assert topk <= CK
    ltri = (lax.broadcasted_iota(I32, (CK, CK), 1) <= lax.broadcasted_iota(I32, (CK, CK), 0)).astype(BF16)
    consts = [p["w_idx_q_t"], p["w_uq_t"], p["w_uk_h"], p["w_uv_t"], ltri]

    def whole(shape):
        nd = len(shape)
        return pl.BlockSpec((1,) + tuple(shape[1:]), lambda i, j: (i,) + (0,) * (nd - 1))

    return pl.pallas_call(
        functools.partial(_mix_dsa_kernel, topk),
        grid=(b, t // TQ),
        in_specs=[pl.BlockSpec((1, TQ, Q_LORA), lambda i, j: (i, j, 0)),
                  pl.BlockSpec((1, SUBLANES, TQ), lambda i, j: (i, 0, j)),
                  whole(ckv.shape), whole(ckvt.shape), whole(kidx.shape)]
        + [_const_spec(c.shape) for c in consts],
        out_specs=pl.BlockSpec((1, TQ, N_HEADS_A * D_V), lambda i, j: (i, j, 0)),
        out_shape=jax.ShapeDtypeStruct((b, t, N_HEADS_A * D_V), BF16),
        scratch_shapes=[pltpu.VMEM((t, TQ), F32), pltpu.VMEM((N_HEADS_A, KV_LORA, TQ), F32),
                        pltpu.VMEM((N_HEADS_A, KV_LORA, TQ), BF16), pltpu.VMEM((N_HEADS_A, CK, TQ), F32),
                        pltpu.VMEM((N_IDX, CK, TQ), F32)],
        compiler_params=pltpu.CompilerParams(
            dimension_semantics=("arbitrary", "arbitrary"), vmem_limit_bytes=VMEM_LIMIT),
        name="mix_dsa",
    )(cq, widxt, ckv, ckvt, kidx, *consts)


def _mixer_params(mix_pre_g, mix_post_g, w_in, q_norm_g, kv_norm_g, w_uq, w_uk, w_uv, w_idx_q,
                  idx_ln_g, idx_ln_b, w_dsa_o, w_dw, b_dw, conv_ln_g, conv_ln_b, w_conv_out,
                  mem_norm_g, w_mem_kv, w_mem_o, w_out):
    hm = N_HEADS_M * D_HEAD_M
    cuts = [0]
    for width in (Q_LORA, KV_LORA, D_IDX, N_IDX, 2 * CONV_CH, hm, 3 * D_MODEL):
        cuts.append(cuts[-1] + width)
    assert w_in.shape == (D_MODEL, cuts[-1])
    w_t = w_in.T
    w_small = jnp.pad(w_t[cuts[2]:cuts[4], :], ((0, LANES - D_IDX - N_IDX), (0, 0)))
    row = lambda v: v.reshape(1, -1).astype(F32)
    return {
        "mix_pre_g": row(mix_pre_g), "mix_post_g": row(mix_post_g),
        "w_cq": w_t[cuts[0]:cuts[1], :].astype(BF16), "w_ckv": w_t[cuts[1]:cuts[2], :].astype(BF16),
        "w_small": w_small.astype(BF16),
        "w_glu": w_t[cuts[4]:cuts[5], :].astype(BF16), "w_qmem": w_t[cuts[5]:cuts[6], :].astype(BF16),
        "w_gate": w_t[cuts[6]:cuts[7], :].astype(BF16),
        "q_norm_g": row(q_norm_g), "kv_norm_g": row(kv_norm_g),
        "idx_ln_g": row(idx_ln_g), "idx_ln_b": row(idx_ln_b),
        "w_dw": w_dw.astype(F32), "b_dw": row(b_dw),
        "conv_ln_g": row(conv_ln_g), "conv_ln_b": row(conv_ln_b),
        "w_conv_out": w_conv_out.astype(BF16), "mem_norm_g": row(mem_norm_g),
        "w_mem_kv": w_mem_kv.astype(BF16), "w_mem_o": w_mem_o.astype(BF16),
        "w_idx_q_t": w_idx_q.T.astype(BF16), "w_uq_t": w_uq.T.astype(BF16),
        "w_uk_h": jnp.transpose(w_uk, (1, 0, 2)).astype(BF16),
        "w_uv_t": jnp.transpose(w_uv, (1, 2, 0)).astype(BF16),
        "w_dsa_o": w_dsa_o.astype(BF16), "w_out": w_out.astype(BF16),
    }


def kernel(x, mem, ffn1_pre_g, ffn1_post_g, ffn1_w_gu, ffn1_w_down, mix_pre_g, mix_post_g, w_in, q_norm_g, kv_norm_g, w_uq, w_uk, w_uv, w_idx_q, idx_ln_g, idx_ln_b, w_dsa_o, w_dw, b_dw, conv_ln_g, conv_ln_b, w_conv_out, mem_norm_g, w_mem_kv, w_mem_o, w_out, ffn2_pre_g, ffn2_post_g, ffn2_w_gu, ffn2_w_down):
    b, t, d = x.shape
    n = b * t
    for l in range(ffn1_pre_g.shape[0]):
        x = _ffn(x.reshape(n, d), ffn1_pre_g[l], ffn1_post_g[l], ffn1_w_gu[l], ffn1_w_down[l])
        p = _mixer_params(mix_pre_g[l], mix_post_g[l], w_in[l], q_norm_g[l], kv_norm_g[l], w_uq[l],
                          w_uk[l], w_uv[l], w_idx_q[l], idx_ln_g[l], idx_ln_b[l], w_dsa_o[l], w_dw[l],
                          b_dw[l], conv_ln_g[l], conv_ln_b[l], w_conv_out[l], mem_norm_g[l],
                          w_mem_kv[l], w_mem_o[l], w_out[l])
        cq, ckv, ckvt, kidx, widxt, ga, mbm = _mix_pre(x.reshape(b, t, d), mem, p)
        oa = _mix_dsa((cq, ckv, ckvt, kidx, widxt), p, t)
        x = _post_ffn(x, oa.reshape(n, -1), ga.reshape(n, d), mbm.reshape(n, d), p,
                      ffn2_pre_g[l], ffn2_post_g[l], ffn2_w_gu[l], ffn2_w_down[l])
        x = x.reshape(b, t, d)
    return x
```
